```python
import math
import jax, jax.numpy as jnp
from jax import lax
import numpy as np

D_MODEL = 2048
BATCH = 4
SEQ = 4096
DEPTH = 4

GRID_W = 64
CTX_LEN = 256
N_MIXERS = 4
N_HEADS = 16
HEAD_DIM = D_MODEL // N_HEADS
N_KV_HEADS = 4
NA_WIN_R = 8
NA_WIN_C = 16
CONV_WIDTH = 31
SHORT_CONV_WIDTH = 3
SWA_WINDOW = 128
SWA_BLOCK = 128
D_FF = ((8 * D_MODEL // 3 + 255) // 256) * 256
ROPE_BASE = 10000.0
ROPE_AXIS_DIM = HEAD_DIM // 2
N_MOD = 9
MACARON_WEIGHT = 0.5
EPS = 1e-6
NEG_INF = -1e30
ATTN_KINDS = (0, 3)
N_NA = (DEPTH + 3) // 4
N_CV = (DEPTH + 2) // 4
N_SC = (DEPTH + 1) // 4
N_SWA = DEPTH // 4

kernel_name = "hybrid_dit_na_conformer_shortconv_swa"


def rms_norm(x, g):
    xf = x.astype(jnp.float32)
    y = xf * lax.rsqrt(jnp.mean(xf * xf, axis=-1, keepdims=True) + EPS)
    return (y * g.astype(jnp.float32)).astype(x.dtype)


def layer_norm(x, g, b):
    xf = x.astype(jnp.float32)
    mu = jnp.mean(xf, axis=-1, keepdims=True)
    var = jnp.mean(jnp.square(xf - mu), axis=-1, keepdims=True)
    y = (xf - mu) * lax.rsqrt(var + EPS)
    return (y * g.astype(jnp.float32) + b.astype(jnp.float32)).astype(x.dtype)


def modulate(h, shift, scale):
    return h * (1 + scale) + shift


def swiglu(h, w_in, w_out):
    gate, up = jnp.split(h @ w_in, 2, axis=-1)
    return (jax.nn.silu(gate) * up) @ w_out


def ffn_sub(h, shift, scale, gate, g, w_in, w_out):
    return h + MACARON_WEIGHT * gate * swiglu(modulate(rms_norm(h, g), shift, scale), w_in, w_out)


def depthwise_conv(h, w, b=None):
    k = w.shape[0]
    pad = k // 2
    y = lax.conv_general_dilated(h, w[:, None, :].astype(h.dtype), window_strides=(1,),
                                 padding=[(pad, pad)], dimension_numbers=("NWC", "WIO", "NWC"),
                                 feature_group_count=h.shape[-1])
    return y if b is None else y + b


def rope_axis(x, ang):
    d2 = x.shape[-1] // 2
    x1, x2 = x[..., :d2], x[..., d2:]
    cos = jnp.cos(ang)[:, None, :]
    sin = jnp.sin(ang)[:, None, :]
    return jnp.concatenate([x1 * cos - x2 * sin, x1 * sin + x2 * cos], axis=-1).astype(x.dtype)


def rope_2d(x, ang_r, ang_c):
    half = x.shape[-1] // 2
    return jnp.concatenate([rope_axis(x[..., :half], ang_r), rope_axis(x[..., half:], ang_c)], axis=-1)


def context_self_attention(qc, kc, vc, sink):
    B, L, KV, G, _ = qc.shape
    s = jnp.einsum('bqkgd,blkd->bkgql', qc, kc).astype(jnp.float32)
    if sink is not None:
        s_sink = jnp.broadcast_to(sink.reshape(KV, G)[None, :, :, None, None].astype(jnp.float32), (B, KV, G, L, 1))
        s = jnp.concatenate([s, s_sink], axis=-1)
    p = jax.nn.softmax(s, axis=-1)[..., :L].astype(vc.dtype)
    o = jnp.einsum('bkgql,blkd->bqkgd', p, vc)
    return o.reshape(B, L, KV * G * qc.shape[-1])


def neighbourhood_attention(u, uc, w_qkv, w_o, rpb, ctx_out):
    B, S, D = u.shape
    L = uc.shape[1]
    rows = S // GRID_W
    wr = min(NA_WIN_R, rows)
    wc = NA_WIN_C
    scale = HEAD_DIM ** -0.5
    qkv = (u @ w_qkv).reshape(B, S, 3, N_HEADS, HEAD_DIM)
    qkv_c = (uc @ w_qkv).reshape(B, L, 3, N_HEADS, HEAD_DIM)
    kc, vc = qkv_c[:, :, 1], qkv_c[:, :, 2]
    qg = (qkv[:, :, 0] * scale).reshape(B, rows, GRID_W, N_HEADS, HEAD_DIM)
    kg = qkv[:, :, 1].reshape(B, rows, GRID_W, N_HEADS, HEAD_DIM)
    vg = qkv[:, :, 2].reshape(B, rows, GRID_W, N_HEADS, HEAD_DIM)
    row_start = jnp.clip(jnp.arange(rows) - wr // 2, 0, rows - wr)
    col_start = jnp.clip(jnp.arange(GRID_W) - wc // 2, 0, GRID_W - wc)
    col_idx = col_start[:, None] + jnp.arange(wc)[None, :]
    col_off = col_idx - jnp.arange(GRID_W)[:, None] + (NA_WIN_C - 1)
    rpb_c = rpb[:, :, col_off]
    n_loc = wr * wc

    def row_block(r):
        rs = row_start[r]
        q_r = lax.dynamic_index_in_dim(qg, r, axis=1, keepdims=False)
        k_rows = lax.dynamic_slice_in_dim(kg, rs, wr, axis=1)
        v_rows = lax.dynamic_slice_in_dim(vg, rs, wr, axis=1)
        k_win = jnp.take(k_rows, col_idx, axis=2)
        v_win = jnp.take(v_rows, col_idx, axis=2)
        s_loc = jnp.einsum('bqhd,brqkhd->bhqrk', q_r, k_win).astype(jnp.float32)
        row_off = rs + jnp.arange(wr) - r + (NA_WIN_R - 1)
        bias = jnp.take(rpb_c, row_off, axis=1).transpose(0, 2, 1, 3)
        s_loc = (s_loc + bias[None].astype(jnp.float32)).reshape(B, N_HEADS, GRID_W, n_loc)
        s_ctx = jnp.einsum('bqhd,blhd->bhql', q_r, kc).astype(jnp.float32)
        p = jax.nn.softmax(jnp.concatenate([s_loc, s_ctx], axis=-1), axis=-1).astype(vg.dtype)
        p_loc = p[..., :n_loc].reshape(B, N_HEADS, GRID_W, wr, wc)
        p_ctx = p[..., n_loc:]
        return (jnp.einsum('bhqrk,brqkhd->bqhd', p_loc, v_win)
                + jnp.einsum('bhql,blhd->bqhd', p_ctx, vc))

    o = lax.map(row_block, jnp.arange(rows))
    y = o.transpose(1, 0, 2, 3, 4).reshape(B, S, D) @ w_o
    yc = None
    if ctx_out:
        qc = (qkv_c[:, :, 0] * scale).reshape(B, L, N_HEADS, 1, HEAD_DIM)
        yc = context_self_attention(qc, kc, vc, None) @ w_o
    return y, yc


def conformer_conv(h, w_pw1, b_pw1, w_dw, b_dw, ln_g, ln_b, w_pw2, b_pw2):
    a, gt = jnp.split(h @ w_pw1 + b_pw1, 2, axis=-1)
    z = a * jax.nn.sigmoid(gt)
    z = depthwise_conv(z, w_dw, b_dw)
    z = jax.nn.silu(layer_norm(z, ln_g, ln_b))
    return z @ w_pw2 + b_pw2


def short_gated_conv(h, w_in, w_conv, w_out):
    bg, cg, xin = jnp.split(h @ w_in, 3, axis=-1)
    return (bg * depthwise_conv(cg * xin, w_conv)) @ w_out


def windowed_gqa_sink(u, uc, w_qkv, w_o, sink, ang_r, ang_c, ctx_out):
    B, S, D = u.shape
    L = uc.shape[1]
    G = N_HEADS // N_KV_HEADS
    kvd = N_KV_HEADS * HEAD_DIM
    scale = HEAD_DIM ** -0.5

    def split_qkv(h, n):
        z = h @ w_qkv
        q = z[..., :D].reshape(B, n, N_HEADS, HEAD_DIM)
        k = z[..., D:D + kvd].reshape(B, n, N_KV_HEADS, HEAD_DIM)
        v = z[..., D + kvd:].reshape(B, n, N_KV_HEADS, HEAD_DIM)
        return q, k, v

    q, k, v = split_qkv(u, S)
    q = rope_2d(q, ang_r, ang_c) * scale
    k = rope_2d(k, ang_r, ang_c)
    qc, kc, vc = split_qkv(uc, L)
    nb = S // SWA_BLOCK
    qb = q.reshape(B, nb, SWA_BLOCK, N_KV_HEADS, G, HEAD_DIM)
    pad = ((0, 0), (SWA_BLOCK, SWA_BLOCK), (0, 0), (0, 0))
    kp = jnp.pad(k, pad).reshape(B, nb + 2, SWA_BLOCK, N_KV_HEADS, HEAD_DIM)
    vp = jnp.pad(v, pad).reshape(B, nb + 2, SWA_BLOCK, N_KV_HEADS, HEAD_DIM)
    kw = jnp.concatenate([kp[:, :-2], kp[:, 1:-1], kp[:, 2:]], axis=2)
    vw = jnp.concatenate([vp[:, :-2], vp[:, 1:-1], vp[:, 2:]], axis=2)
    qpos = jnp.arange(nb)[:, None] * SWA_BLOCK + jnp.arange(SWA_BLOCK)[None, :]
    kpos = jnp.arange(nb)[:, None] * SWA_BLOCK - SWA_BLOCK + jnp.arange(3 * SWA_BLOCK)[None, :]
    mask = ((jnp.abs(kpos[:, None, :] - qpos[:, :, None]) <= SWA_WINDOW)
            & (kpos[:, None, :] >= 0) & (kpos[:, None, :] < S))
    s_loc = jnp.einsum('bnqkgd,bnjkd->bnkgqj', qb, kw).astype(jnp.float32)
    s_loc = jnp.where(mask[None, :, None, None], s_loc, NEG_INF)
    s_ctx = jnp.einsum('bnqkgd,blkd->bnkgql', qb, kc).astype(jnp.float32)
    s_sink = jnp.broadcast_to(sink.reshape(N_KV_HEADS, G)[None, None, :, :, None, None].astype(jnp.float32),
                              (B, nb, N_KV_HEADS, G, SWA_BLOCK, 1))
    p = jax.nn.softmax(jnp.concatenate([s_loc, s_ctx, s_sink], axis=-1), axis=-1).astype(v.dtype)
    n_loc = 3 * SWA_BLOCK
    o = (jnp.einsum('bnkgqj,bnjkd->bnqkgd', p[..., :n_loc], vw)
         + jnp.einsum('bnkgql,blkd->bnqkgd', p[..., n_loc:n_loc + L], vc))
    y = o.reshape(B, S, D) @ w_o
    yc = None
    if ctx_out:
        qcg = (qc * scale).reshape(B, L, N_KV_HEADS, G, HEAD_DIM)
        yc = context_self_attention(qcg, kc, vc, sink) @ w_o
    return y, yc


def setup_inputs(seed: int = 0) -> dict:
    key = jax.random.key(seed)
    ks = jax.random.split(key, 27)
    D, F, H = D_MODEL, D_FF, N_HEADS
    kvd = N_KV_HEADS * HEAD_DIM

    def nrm(k, shape, s):
        return jax.random.normal(k, shape, jnp.float32) * s

    return {
        "x": nrm(ks[0], (BATCH, SEQ, D), 1.0),
        "c": nrm(ks[1], (BATCH, D), 1.0),
        "ctx": nrm(ks[2], (BATCH, CTX_LEN, D), 1.0),
        "c_ctx": nrm(ks[3], (D,), 1.0),
        "w_mod": nrm(ks[4], (DEPTH, D, N_MOD * D), D ** -0.5),
        "b_mod": nrm(ks[5], (DEPTH, N_MOD * D), 0.02),
        "norm_g": 1.0 + nrm(ks[6], (DEPTH, 3, D), 0.02),
        "ffn_w_in": nrm(ks[7], (DEPTH, 2, D, 2 * F), D ** -0.5),
        "ffn_w_out": nrm(ks[8], (DEPTH, 2, F, D), F ** -0.5),
        "na_w_qkv": nrm(ks[9], (N_NA, D, 3 * D), D ** -0.5),
        "na_w_o": nrm(ks[10], (N_NA, D, D), D ** -0.5),
        "na_rpb": nrm(ks[11], (N_NA, H, 2 * NA_WIN_R - 1, 2 * NA_WIN_C - 1), 0.1),
        "cv_w_pw1": nrm(ks[12], (N_CV, D, 2 * D), D ** -0.5),
        "cv_b_pw1": nrm(ks[13], (N_CV, 2 * D), 0.02),
        "cv_w_dw": nrm(ks[14], (N_CV, CONV_WIDTH, D), CONV_WIDTH ** -0.5),
        "cv_b_dw": nrm(ks[15], (N_CV, D), 0.02),
        "cv_ln_g": 1.0 + nrm(ks[16], (N_CV, D), 0.02),
        "cv_ln_b": nrm(ks[17], (N_CV, D), 0.02),
        "cv_w_pw2": nrm(ks[18], (N_CV, D, D), D ** -0.5),
        "cv_b_pw2": nrm(ks[19], (N_CV, D), 0.02),
        "sc_w_in": nrm(ks[20], (N_SC, D, 3 * D), D ** -0.5),
        "sc_w_conv": nrm(ks[21], (N_SC, SHORT_CONV_WIDTH, D), SHORT_CONV_WIDTH ** -0.5),
        "sc_w_out": nrm(ks[22], (N_SC, D, D), D ** -0.5),
        "swa_w_qkv": nrm(ks[23], (N_SWA, D, D + 2 * kvd), D ** -0.5),
        "swa_w_o": nrm(ks[24], (N_SWA, D, D), D ** -0.5),
        "swa_sink": nrm(ks[25], (N_SWA, H), 1.0),
        "final_g": 1.0 + nrm(ks[26], (D,), 0.02),
    }


def reference(x, c, ctx, c_ctx, w_mod, b_mod, norm_g, ffn_w_in, ffn_w_out,
              na_w_qkv, na_w_o, na_rpb,
              cv_w_pw1, cv_b_pw1, cv_w_dw, cv_b_dw, cv_ln_g, cv_ln_b, cv_w_pw2, cv_b_pw2,
              sc_w_in, sc_w_conv, sc_w_out,
              swa_w_qkv, swa_w_o, swa_sink, final_g):
    B, S, D = x.shape
    t = jnp.arange(S)
    inv_freq = jnp.power(ROPE_BASE, -jnp.arange(ROPE_AXIS_DIM // 2, dtype=jnp.float32) / (ROPE_AXIS_DIM // 2))
    ang_r = (t // GRID_W).astype(jnp.float32)[:, None] * inv_freq[None, :]
    ang_c = (t % GRID_W).astype(jnp.float32)[:, None] * inv_freq[None, :]
    silu_c = jax.nn.silu(c)
    silu_cc = jax.nn.silu(c_ctx)

    for i in range(DEPTH):
        kind = i % N_MIXERS
        j = i // N_MIXERS
        last = i == DEPTH - 1
        ml = (silu_c @ w_mod[i] + b_mod[i]).reshape(B, N_MOD, D)[:, :, None, :]
        mc = (silu_cc @ w_mod[i] + b_mod[i]).reshape(N_MOD, D)
        g = norm_g[i]
        need_ctx = (not last) or (kind in ATTN_KINDS)

        x = ffn_sub(x, ml[:, 0], ml[:, 1], ml[:, 2], g[0], ffn_w_in[i, 0], ffn_w_out[i, 0])
        if need_ctx:
            ctx = ffn_sub(ctx, mc[0], mc[1], mc[2], g[0], ffn_w_in[i, 0], ffn_w_out[i, 0])

        u = modulate(rms_norm(x, g[1]), ml[:, 3], ml[:, 4])
        uc = modulate(rms_norm(ctx, g[1]), mc[3], mc[4]) if need_ctx else None
        if kind == 0:
            y, yc = neighbourhood_attention(u, uc, na_w_qkv[j], na_w_o[j], na_rpb[j], not last)
        elif kind == 1:
            cv = (cv_w_pw1[j], cv_b_pw1[j], cv_w_dw[j], cv_b_dw[j], cv_ln_g[j], cv_ln_b[j], cv_w_pw2[j], cv_b_pw2[j])
            y = conformer_conv(u, *cv)
            yc = None if last else conformer_conv(uc, *cv)
        elif kind == 2:
            y = short_gated_conv(u, sc_w_in[j], sc_w_conv[j], sc_w_out[j])
            yc = None if last else short_gated_conv(uc, sc_w_in[j], sc_w_conv[j], sc_w_out[j])
        else:
            y, yc = windowed_gqa_sink(u, uc, swa_w_qkv[j], swa_w_o[j], swa_sink[j], ang_r, ang_c, not last)
        x = x + ml[:, 5] * y

        x = ffn_sub(x, ml[:, 6], ml[:, 7], ml[:, 8], g[2], ffn_w_in[i, 1], ffn_w_out[i, 1])
        if not last:
            ctx = ctx + mc[5] * yc
            ctx = ffn_sub(ctx, mc[6], mc[7], mc[8], g[2], ffn_w_in[i, 1], ffn_w_out[i, 1])

    return rms_norm(x, final_g)
```

```python
import functools
import math

import jax
import jax.numpy as jnp
from jax import lax
from jax.experimental import pallas as pl
from jax.experimental.pallas import tpu as pltpu

D_MODEL = 2048
BATCH = 4
SEQ = 4096
DEPTH = 4
GRID_W = 64
CTX_LEN = 256
N_HEADS = 16
HEAD_DIM = D_MODEL // N_HEADS
N_KV_HEADS = 4
NA_WIN_R = 8
NA_WIN_C = 16
CONV_WIDTH = 31
SHORT_CONV_WIDTH = 3
SWA_WINDOW = 128
SWA_BLOCK = 128
D_FF = ((8 * D_MODEL // 3 + 255) // 256) * 256
ROPE_BASE = 10000.0
ROPE_AXIS_DIM = HEAD_DIM // 2
N_MOD = 9
MACARON_WEIGHT = 0.5
EPS = 1e-6
NEG_INF = -1e30

R_LAT = BATCH * SEQ
R_CTX = BATCH * CTX_LEN
R_ALL = R_LAT + R_CTX
MOD_ROWS = 8
ATTN_SCALE = HEAD_DIM ** -0.5

F32 = jnp.float32
BF16 = jnp.bfloat16

VMEM_LIMIT_BYTES = 56 * 1024 * 1024

TM = 512
TF = 512
CONV_TILE = 256


def _cparams(sem):
    return pltpu.CompilerParams(dimension_semantics=sem, vmem_limit_bytes=VMEM_LIMIT_BYTES)


def _mod_row(i, tm):
    return jnp.minimum(i // (SEQ // tm), BATCH)


def _mod_spec(layer, k, tm, width=D_MODEL, col=None):
    if col is None:
        return pl.BlockSpec((None, None, None, 1, width),
                            lambda i, j: (layer, _mod_row(i, tm), k, 0, 0))
    return pl.BlockSpec((None, None, None, 1, width),
                        lambda i, j: (layer, _mod_row(i, tm), k, 0, col(j)))


def _rms(x, g):
    return x * lax.rsqrt(jnp.mean(x * x, axis=-1, keepdims=True) + EPS) * g


def _norm_mod(x, g, shift, scale):
    return _rms(x, g) * (1.0 + scale) + shift


def _silu(x):
    return x * jax.nn.sigmoid(x)


def _mods_kernel(cc_ref, w_ref, b_ref, o_ref):
    s = _silu(cc_ref[...]).astype(BF16)
    o_ref[...] = jnp.dot(s, w_ref[...].astype(BF16), preferred_element_type=F32) + b_ref[...]


def _mods(cc, w_mod, b_mod):
    tn = 1024
    n = N_MOD * D_MODEL
    return pl.pallas_call(
        _mods_kernel,
        grid=(DEPTH, n // tn),
        in_specs=[
            pl.BlockSpec((MOD_ROWS, D_MODEL), lambda l, j: (0, 0)),
            pl.BlockSpec((None, D_MODEL, tn), lambda l, j: (l, 0, j)),
            pl.BlockSpec((None, 1, tn), lambda l, j: (l, 0, j)),
        ],
        out_specs=pl.BlockSpec((None, MOD_ROWS, tn), lambda l, j: (l, 0, j)),
        out_shape=jax.ShapeDtypeStruct((DEPTH, MOD_ROWS, n), F32),
        compiler_params=_cparams(("parallel", "parallel")),
        name="mods",
    )(cc, w_mod, b_mod.reshape(DEPTH, 1, n))


def _ffn_kernel(*refs, final):
    if final:
        x_ref, g_ref, sh_ref, sc_ref, gt_ref, wg_ref, wu_ref, wo_ref, fg_ref, o_ref, h_ref = refs
    else:
        x_ref, g_ref, sh_ref, sc_ref, gt_ref, wg_ref, wu_ref, wo_ref, o_ref, h_ref = refs
    f = pl.program_id(1)
    nf = pl.num_programs(1)

    @pl.when(f == 0)
    def _():
        h_ref[...] = _norm_mod(x_ref[...], g_ref[...], sh_ref[...], sc_ref[...]).astype(BF16)

    h = h_ref[...]
    gate = jnp.dot(h, wg_ref[...], preferred_element_type=F32)
    up = jnp.dot(h, wu_ref[...], preferred_element_type=F32)
    a = (_silu(gate) * up).astype(BF16)
    part = jnp.dot(a, wo_ref[...], preferred_element_type=F32)

    @pl.when(f == 0)
    def _():
        o_ref[...] = part

    @pl.when(f > 0)
    def _():
        o_ref[...] += part

    @pl.when(f == nf - 1)
    def _():
        y = x_ref[...] + MACARON_WEIGHT * gt_ref[...] * o_ref[...]
        if final:
            y = _rms(y, fg_ref[...])
        o_ref[...] = y


def _ffn(x, mods, layer, k0, g, w_in, w_out, rows, final_g=None):
    tm, tf = TM, TF
    nfc = D_FF // tf
    final = final_g is not None
    in_specs = [
        pl.BlockSpec((tm, D_MODEL), lambda i, f: (i, 0)),
        pl.BlockSpec((1, D_MODEL), lambda i, f: (0, 0)),
        _mod_spec(layer, k0, tm),
        _mod_spec(layer, k0 + 1, tm),
        _mod_spec(layer, k0 + 2, tm),
        pl.BlockSpec((D_MODEL, tf), lambda i, f: (0, f)),
        pl.BlockSpec((D_MODEL, tf), lambda i, f: (0, nfc + f)),
        pl.BlockSpec((tf, D_MODEL), lambda i, f: (f, 0)),
    ]
    args = [x, g, mods, mods, mods, w_in, w_in, w_out]
    if final:
        in_specs.append(pl.BlockSpec((1, D_MODEL), lambda i, f: (0, 0)))
        args.append(final_g)
    return pl.pallas_call(
        functools.partial(_ffn_kernel, final=final),
        grid=(rows // tm, nfc),
        in_specs=in_specs,
        out_specs=pl.BlockSpec((tm, D_MODEL), lambda i, f: (i, 0)),
        out_shape=jax.ShapeDtypeStruct((rows, D_MODEL), F32),
        scratch_shapes=[pltpu.VMEM((tm, D_MODEL), BF16)],
        compiler_params=_cparams(("parallel", "arbitrary")),
        name="ffn_final" if final else "ffn",
    )(*args)


def _proj_prologue(x_ref, g_ref, sh_ref, sc_ref, h_ref):
    @pl.when(pl.program_id(1) == 0)
    def _():
        h_ref[...] = _norm_mod(x_ref[...], g_ref[...], sh_ref[...], sc_ref[...]).astype(BF16)


def _proj_kernel(x_ref, g_ref, sh_ref, sc_ref, w_ref, o_ref, h_ref):
    _proj_prologue(x_ref, g_ref, sh_ref, sc_ref, h_ref)
    o_ref[...] = jnp.dot(h_ref[...], w_ref[...], preferred_element_type=F32).astype(o_ref.dtype)


def _rope_swap(y):
    n = y.shape[-1]
    lane = lax.broadcasted_iota(jnp.int32, y.shape, 1)
    return jnp.where(lane % ROPE_AXIS_DIM < ROPE_AXIS_DIM // 2,
                     pltpu.roll(y, n - ROPE_AXIS_DIM // 2, 1),
                     pltpu.roll(y, ROPE_AXIS_DIM // 2, 1))


def _proj_rope_kernel(x_ref, g_ref, sh_ref, sc_ref, w_ref, cos_ref, sin_ref, o_ref, h_ref, *, n_rope):
    _proj_prologue(x_ref, g_ref, sh_ref, sc_ref, h_ref)
    j = pl.program_id(1)
    y = jnp.dot(h_ref[...], w_ref[...], preferred_element_type=F32)

    @pl.when(j < n_rope)
    def _():
        cos = cos_ref[...]
        sin = sin_ref[...]
        for hh in range(y.shape[-1] // HEAD_DIM):
            yh = y[:, hh * HEAD_DIM:(hh + 1) * HEAD_DIM]
            o_ref[:, hh * HEAD_DIM:(hh + 1) * HEAD_DIM] = (yh * cos + _rope_swap(yh) * sin).astype(o_ref.dtype)

    @pl.when(j >= n_rope)
    def _():
        o_ref[...] = y.astype(o_ref.dtype)


def _proj_common_specs(layer, tm):
    return [
        pl.BlockSpec((tm, D_MODEL), lambda i, j: (i, 0)),
        pl.BlockSpec((1, D_MODEL), lambda i, j: (0, 0)),
        _mod_spec(layer, 3, tm),
        _mod_spec(layer, 4, tm),
    ]


def _proj(x, mods, layer, g, w, tn):
    tm = TM
    n = w.shape[1]
    return pl.pallas_call(
        _proj_kernel,
        grid=(R_ALL // tm, n // tn),
        in_specs=_proj_common_specs(layer, tm) + [pl.BlockSpec((D_MODEL, tn), lambda i, j: (0, j))],
        out_specs=pl.BlockSpec((tm, tn), lambda i, j: (i, j)),
        out_shape=jax.ShapeDtypeStruct((R_ALL, n), BF16),
        scratch_shapes=[pltpu.VMEM((tm, D_MODEL), BF16)],
        compiler_params=_cparams(("parallel", "arbitrary")),
        name="proj",
    )(x, g, mods, mods, w)


def _proj_rope(x, mods, layer, g, w, cos_t, sin_t, n_rope_cols):
    tm, tn = TM, 512
    n = w.shape[1]
    n_lat = R_LAT // tm
    tpb = SEQ // tm

    def tab(i, j):
        return (jnp.where(i < n_lat, i % tpb, tpb + i - n_lat), 0)

    return pl.pallas_call(
        functools.partial(_proj_rope_kernel, n_rope=n_rope_cols // tn),
        grid=(R_ALL // tm, n // tn),
        in_specs=_proj_common_specs(layer, tm) + [
            pl.BlockSpec((D_MODEL, tn), lambda i, j: (0, j)),
            pl.BlockSpec((tm, HEAD_DIM), tab),
            pl.BlockSpec((tm, HEAD_DIM), tab),
        ],
        out_specs=pl.BlockSpec((tm, tn), lambda i, j: (i, j)),
        out_shape=jax.ShapeDtypeStruct((R_ALL, n), BF16),
        scratch_shapes=[pltpu.VMEM((tm, D_MODEL), BF16)],
        compiler_params=_cparams(("parallel", "arbitrary")),
        name="proj_rope",
    )(x, g, mods, mods, w, cos_t, sin_t)


def _proj_glu_kernel(x_ref, g_ref, sh_ref, sc_ref, wa_ref, wg_ref, ba_ref, bg_ref, o_ref, h_ref):
    _proj_prologue(x_ref, g_ref, sh_ref, sc_ref, h_ref)
    h = h_ref[...]
    a = jnp.dot(h, wa_ref[...], preferred_element_type=F32) + ba_ref[...]
    gt = jnp.dot(h, wg_ref[...], preferred_element_type=F32) + bg_ref[...]
    o_ref[...] = a * jax.nn.sigmoid(gt)


def _proj_glu(x, mods, layer, g, w, b):
    tm, tn = TM, 512
    nc = D_MODEL // tn
    return pl.pallas_call(
        _proj_glu_kernel,
        grid=(R_ALL // tm, nc),
        in_specs=_proj_common_specs(layer, tm) + [
            pl.BlockSpec((D_MODEL, tn), lambda i, j: (0, j)),
            pl.BlockSpec((D_MODEL, tn), lambda i, j: (0, nc + j)),
            pl.BlockSpec((1, tn), lambda i, j: (0, j)),
            pl.BlockSpec((1, tn), lambda i, j: (0, nc + j)),
        ],
        out_specs=pl.BlockSpec((tm, tn), lambda i, j: (i, j)),
        out_shape=jax.ShapeDtypeStruct((R_ALL, D_MODEL), F32),
        scratch_shapes=[pltpu.VMEM((tm, D_MODEL), BF16)],
        compiler_params=_cparams(("parallel", "arbitrary")),
        name="proj_glu",
    )(x, g, mods, mods, w, w, b, b)


def _proj_sc_kernel(x_ref, g_ref, sh_ref, sc_ref, wb_ref, wc_ref, wx_ref, bg_ref, p_ref, h_ref):
    _proj_prologue(x_ref, g_ref, sh_ref, sc_ref, h_ref)
    h = h_ref[...]
    bg_ref[...] = jnp.dot(h, wb_ref[...], preferred_element_type=F32).astype(bg_ref.dtype)
    cg = jnp.dot(h, wc_ref[...], preferred_element_type=F32)
    xin = jnp.dot(h, wx_ref[...], preferred_element_type=F32)
    p_ref[...] = cg * xin


def _proj_sc(x, mods, layer, g, w):
    tm, tn = TM, 512
    nc = D_MODEL // tn
    return pl.pallas_call(
        _proj_sc_kernel,
        grid=(R_ALL // tm, nc),
        in_specs=_proj_common_specs(layer, tm) + [
            pl.BlockSpec((D_MODEL, tn), lambda i, j: (0, j)),
            pl.BlockSpec((D_MODEL, tn), lambda i, j: (0, nc + j)),
            pl.BlockSpec((D_MODEL, tn), lambda i, j: (0, 2 * nc + j)),
        ],
        out_specs=[pl.BlockSpec((tm, tn), lambda i, j: (i, j)),
                   pl.BlockSpec((tm, tn), lambda i, j: (i, j))],
        out_shape=[jax.ShapeDtypeStruct((R_ALL, D_MODEL), BF16),
                   jax.ShapeDtypeStruct((R_ALL, D_MODEL), F32)],
        scratch_shapes=[pltpu.VMEM((tm, D_MODEL), BF16)],
        compiler_params=_cparams(("parallel", "arbitrary")),
        name="proj_sc",
    )(x, g, mods, mods, w, w, w)


def _outproj_kernel(*refs, n_lat, has_ctx, has_bias):
    refs = list(refs)
    ol_ref = refs.pop(0)
    oc_ref = refs.pop(0) if has_ctx else None
    w_ref = refs.pop(0)
    b_ref = refs.pop(0) if has_bias else None
    x_ref, gt_ref, o_ref = refs
    i = pl.program_id(0)

    def finish(src_ref):
        y = jnp.dot(src_ref[...], w_ref[...], preferred_element_type=F32)
        if has_bias:
            y = y + b_ref[...]
        o_ref[...] = x_ref[...] + gt_ref[...] * y

    if has_ctx:
        @pl.when(i < n_lat)
        def _():
            finish(ol_ref)

        @pl.when(i >= n_lat)
        def _():
            finish(oc_ref)
    else:
        finish(ol_ref)


def _outproj(o_lat, o_ctx, x, mods, layer, w, bias, rows):
    tm, tn = TM, 1024
    n_lat = R_LAT // tm
    has_ctx = o_ctx is not None
    has_bias = bias is not None
    in_specs = []
    args = []
    if has_ctx:
        in_specs.append(pl.BlockSpec((tm, D_MODEL), lambda i, j: (jnp.minimum(i, n_lat - 1), 0)))
        in_specs.append(pl.BlockSpec((tm, D_MODEL), lambda i, j: (jnp.maximum(i - n_lat, 0), 0)))
        args += [o_lat, o_ctx]
    else:
        in_specs.append(pl.BlockSpec((tm, D_MODEL), lambda i, j: (i, 0)))
        args.append(o_lat)
    in_specs.append(pl.BlockSpec((D_MODEL, tn), lambda i, j: (0, j)))
    args.append(w)
    if has_bias:
        in_specs.append(pl.BlockSpec((1, tn), lambda i, j: (0, j)))
        args.append(bias)
    in_specs.append(pl.BlockSpec((tm, tn), lambda i, j: (i, j)))
    in_specs.append(_mod_spec(layer, 5, tm, width=tn, col=lambda j: j))
    args += [x, mods]
    return pl.pallas_call(
        functools.partial(_outproj_kernel, n_lat=n_lat, has_ctx=has_ctx, has_bias=has_bias),
        grid=(rows // tm, D_MODEL // tn),
        in_specs=in_specs,
        out_specs=pl.BlockSpec((tm, tn), lambda i, j: (i, j)),
        out_shape=jax.ShapeDtypeStruct((rows, D_MODEL), F32),
        compiler_params=_cparams(("parallel", "parallel")),
        name="outproj",
    )(*args)


NA_KEYS = NA_WIN_R * GRID_W
NA_CFGS = NA_WIN_R


def _na_bias_table(rpb):
    cfg = jnp.arange(NA_CFGS)[:, None, None, None]
    c = jnp.arange(GRID_W)[None, :, None, None]
    jr = jnp.arange(NA_WIN_R)[None, None, :, None]
    kc = jnp.arange(GRID_W)[None, None, None, :]
    cs = jnp.clip(c - NA_WIN_C // 2, 0, GRID_W - NA_WIN_C)
    valid = (kc >= cs) & (kc < cs + NA_WIN_C)
    row_off = jnp.broadcast_to(jr - cfg + (NA_WIN_R - 1), (NA_CFGS, GRID_W, NA_WIN_R, GRID_W))
    col_off = jnp.broadcast_to(jnp.clip(kc - c + (NA_WIN_C - 1), 0, 2 * NA_WIN_C - 2),
                               (NA_CFGS, GRID_W, NA_WIN_R, GRID_W))
    vals = rpb[:, row_off, col_off]
    vals = jnp.where(valid[None], vals, NEG_INF)
    return vals.transpose(1, 0, 2, 3, 4).reshape(NA_CFGS, N_HEADS, GRID_W, NA_KEYS).astype(F32)


_NT = (((1,), (1,)), ((), ()))


def _na_kernel(q_ref, k_ref, v_ref, kc_ref, vc_ref, bias_ref, o_ref):
    rows = SEQ // GRID_W
    kc = kc_ref[...]
    vc = vc_ref[...]

    def body(r, carry):
        rs = jnp.clip(r - NA_WIN_R // 2, 0, rows - NA_WIN_R)
        q0 = pl.multiple_of(r * GRID_W, GRID_W)
        k0 = pl.multiple_of(rs * GRID_W, GRID_W)
        q = q_ref[pl.ds(q0, GRID_W), :]
        kw = k_ref[pl.ds(k0, NA_KEYS), :]
        vw = v_ref[pl.ds(k0, NA_KEYS), :]
        s_loc = lax.dot_general(q, kw, _NT, preferred_element_type=F32) * ATTN_SCALE + bias_ref[r - rs]
        s_ctx = lax.dot_general(q, kc, _NT, preferred_element_type=F32) * ATTN_SCALE
        m = jnp.maximum(jnp.max(s_loc, axis=-1, keepdims=True), jnp.max(s_ctx, axis=-1, keepdims=True))
        p_loc = jnp.exp(s_loc - m)
        p_ctx = jnp.exp(s_ctx - m)
        l = jnp.sum(p_loc, axis=-1, keepdims=True) + jnp.sum(p_ctx, axis=-1, keepdims=True)
        o = (jnp.dot(p_loc.astype(BF16), vw, preferred_element_type=F32)
             + jnp.dot(p_ctx.astype(BF16), vc, preferred_element_type=F32))
        o_ref[pl.ds(q0, GRID_W), :] = (o / l).astype(o_ref.dtype)
        return carry

    lax.fori_loop(0, rows, body, 0)


def _na_attention(qkv, bias_tab):
    h3 = N_HEADS
    ctx_blk = R_LAT // CTX_LEN
    return pl.pallas_call(
        _na_kernel,
        grid=(BATCH, N_HEADS),
        in_specs=[
            pl.BlockSpec((SEQ, HEAD_DIM), lambda b, h: (b, h)),
            pl.BlockSpec((SEQ, HEAD_DIM), lambda b, h: (b, h3 + h)),
            pl.BlockSpec((SEQ, HEAD_DIM), lambda b, h: (b, 2 * h3 + h)),
            pl.BlockSpec((CTX_LEN, HEAD_DIM), lambda b, h: (ctx_blk + b, h3 + h)),
            pl.BlockSpec((CTX_LEN, HEAD_DIM), lambda b, h: (ctx_blk + b, 2 * h3 + h)),
            pl.BlockSpec((NA_CFGS, None, GRID_W, NA_KEYS), lambda b, h: (0, h, 0, 0)),
        ],
        out_specs=pl.BlockSpec((SEQ, HEAD_DIM), lambda b, h: (b, h)),
        out_shape=jax.ShapeDtypeStruct((R_LAT, D_MODEL), BF16),
        compiler_params=_cparams(("parallel", "parallel")),
        name="na_attn",
    )(qkv, qkv, qkv, qkv, qkv, bias_tab)


def _ctx_attn_kernel(q_ref, k_ref, v_ref, o_ref):
    s = lax.dot_general(q_ref[...], k_ref[...], _NT, preferred_element_type=F32) * ATTN_SCALE
    p = jnp.exp(s - jnp.max(s, axis=-1, keepdims=True))
    l = jnp.sum(p, axis=-1, keepdims=True)
    o = jnp.dot(p.astype(BF16), v_ref[...], preferred_element_type=F32)
    o_ref[...] = (o / l).astype(o_ref.dtype)


def _ctx_attention(qkv):
    h3 = N_HEADS
    ctx_blk = R_LAT // CTX_LEN
    return pl.pallas_call(
        _ctx_attn_kernel,
        grid=(BATCH, N_HEADS),
        in_specs=[
            pl.BlockSpec((CTX_LEN, HEAD_DIM), lambda b, h: (ctx_blk + b, h)),
            pl.BlockSpec((CTX_LEN, HEAD_DIM), lambda b, h: (ctx_blk + b, h3 + h)),
            pl.BlockSpec((CTX_LEN, HEAD_DIM), lambda b, h: (ctx_blk + b, 2 * h3 + h)),
        ],
        out_specs=pl.BlockSpec((CTX_LEN, HEAD_DIM), lambda b, h: (b, h)),
        out_shape=jax.ShapeDtypeStruct((R_CTX, D_MODEL), BF16),
        compiler_params=_cparams(("parallel", "parallel")),
        name="ctx_attn",
    )(qkv, qkv, qkv)


SWA_G = N_HEADS // N_KV_HEADS
SWA_KEYS = 3 * SWA_BLOCK


def _swa_kernel(q_ref, k_ref, v_ref, kc_ref, vc_ref, sink_ref, o_ref):
    n = pl.program_id(2)
    ws = jnp.clip((n - 1) * SWA_BLOCK, 0, SEQ - SWA_KEYS)
    ws = pl.multiple_of(ws, SWA_BLOCK)
    q = jnp.concatenate([q_ref[:, g * HEAD_DIM:(g + 1) * HEAD_DIM] for g in range(SWA_G)], axis=0)
    kw = k_ref[pl.ds(ws, SWA_KEYS), :]
    vw = v_ref[pl.ds(ws, SWA_KEYS), :]
    s_loc = lax.dot_general(q, kw, _NT, preferred_element_type=F32) * ATTN_SCALE
    qpos = n * SWA_BLOCK + lax.broadcasted_iota(jnp.int32, s_loc.shape, 0) % SWA_BLOCK
    kpos = ws + lax.broadcasted_iota(jnp.int32, s_loc.shape, 1)
    s_loc = jnp.where(jnp.abs(kpos - qpos) <= SWA_WINDOW, s_loc, NEG_INF)
    s_ctx = lax.dot_general(q, kc_ref[...], _NT, preferred_element_type=F32) * ATTN_SCALE
    sink = sink_ref[...]
    m = jnp.maximum(jnp.maximum(jnp.max(s_loc, axis=-1, keepdims=True),
                                jnp.max(s_ctx, axis=-1, keepdims=True)), sink)
    p_loc = jnp.exp(s_loc - m)
    p_ctx = jnp.exp(s_ctx - m)
    l = (jnp.sum(p_loc, axis=-1, keepdims=True) + jnp.sum(p_ctx, axis=-1, keepdims=True)
         + jnp.exp(sink - m))
    o = (jnp.dot(p_loc.astype(BF16), vw, preferred_element_type=F32)
         + jnp.dot(p_ctx.astype(BF16), vc_ref[...], preferred_element_type=F32)) / l
    for g in range(SWA_G):
        o_ref[:, g * HEAD_DIM:(g + 1) * HEAD_DIM] = o[g * SWA_BLOCK:(g + 1) * SWA_BLOCK].astype(o_ref.dtype)


def _swa_attention(qkv, sink_rows):
    nb = SEQ // SWA_BLOCK
    kcol = N_HEADS
    vcol = N_HEADS + N_KV_HEADS
    ctx_blk = R_LAT // CTX_LEN
    return pl.pallas_call(
        _swa_kernel,
        grid=(BATCH, N_KV_HEADS, nb),
        in_specs=[
            pl.BlockSpec((SWA_BLOCK, SWA_G * HEAD_DIM), lambda b, kv, n: (b * nb + n, kv)),
            pl.BlockSpec((SEQ, HEAD_DIM), lambda b, kv, n: (b, kcol + kv)),
            pl.BlockSpec((SEQ, HEAD_DIM), lambda b, kv, n: (b, vcol + kv)),
            pl.BlockSpec((CTX_LEN, HEAD_DIM), lambda b, kv, n: (ctx_blk + b, kcol + kv)),
            pl.BlockSpec((CTX_LEN, HEAD_DIM), lambda b, kv, n: (ctx_blk + b, vcol + kv)),
            pl.BlockSpec((None, SWA_G * SWA_BLOCK, 1), lambda b, kv, n: (kv, 0, 0)),
        ],
        out_specs=pl.BlockSpec((SWA_BLOCK, SWA_G * HEAD_DIM), lambda b, kv, n: (b * nb + n, kv)),
        out_shape=jax.ShapeDtypeStruct((R_LAT, D_MODEL), BF16),
        compiler_params=_cparams(("parallel", "parallel", "arbitrary")),
        name="swa_attn",
    )(qkv, qkv, qkv, qkv, qkv, sink_rows)


CONV_HALO = 16
CONV_LANES = 128
CONV_ROWS = 64


def _halo_flags(i):
    tiles_per_seq = SEQ // CONV_TILE
    latent = i < R_LAT // CONV_TILE
    has_prev = jnp.logical_and(latent, i % tiles_per_seq != 0)
    has_next = jnp.logical_and(latent, i % tiles_per_seq != tiles_per_seq - 1)
    return has_prev, has_next


def _stage_rows(prev_ref, cur_ref, next_ref, stage_ref, i):
    has_prev, has_next = _halo_flags(i)
    stage_ref[0:CONV_HALO, :] = jnp.where(has_prev, prev_ref[...], 0.0)
    stage_ref[CONV_HALO:CONV_HALO + CONV_TILE, :] = cur_ref[...]
    stage_ref[CONV_HALO + CONV_TILE:, :] = jnp.where(has_next, next_ref[...], 0.0)


def _dwconv(stage_ref, w_ref, width, emit):
    pad = width // 2

    def lane_chunk(ci, carry):
        lanes = pl.ds(pl.multiple_of(ci * CONV_LANES, CONV_LANES), CONV_LANES)
        w = w_ref[:, lanes]
        for r0 in range(0, CONV_TILE, CONV_ROWS):
            acc = None
            for k in range(width):
                start = CONV_HALO + r0 + k - pad
                term = stage_ref[start:start + CONV_ROWS, lanes] * w[k:k + 1, :]
                acc = term if acc is None else acc + term
            emit(r0, lanes, acc)
        return carry

    lax.fori_loop(0, D_MODEL // CONV_LANES, lane_chunk, 0)


def _cv_conv_kernel(prev_ref, cur_ref, next_ref, w_ref, b_ref, lg_ref, lb_ref, o_ref, stage_ref, y_ref):
    i = pl.program_id(0)
    _stage_rows(prev_ref, cur_ref, next_ref, stage_ref, i)

    def emit(r0, lanes, y):
        y_ref[r0:r0 + CONV_ROWS, lanes] = y + b_ref[:, lanes]

    _dwconv(stage_ref, w_ref, CONV_WIDTH, emit)
    y = y_ref[...]
    mu = jnp.mean(y, axis=-1, keepdims=True)
    yc = y - mu
    var = jnp.mean(yc * yc, axis=-1, keepdims=True)
    z = yc * lax.rsqrt(var + EPS) * lg_ref[...] + lb_ref[...]
    o_ref[...] = _silu(z).astype(o_ref.dtype)


def _halo_specs():
    per = CONV_TILE // CONV_HALO
    last = R_ALL // CONV_HALO - 1
    return [
        pl.BlockSpec((CONV_HALO, D_MODEL), lambda i: (jnp.maximum(i * per - 1, 0), 0)),
        pl.BlockSpec((CONV_TILE, D_MODEL), lambda i: (i, 0)),
        pl.BlockSpec((CONV_HALO, D_MODEL), lambda i: (jnp.minimum((i + 1) * per, last), 0)),
    ]


def _cv_conv(z, w_dw, b_dw, ln_g, ln_b):
    vec = pl.BlockSpec((1, D_MODEL), lambda i: (0, 0))
    return pl.pallas_call(
        _cv_conv_kernel,
        grid=(R_ALL // CONV_TILE,),
        in_specs=_halo_specs() + [pl.BlockSpec((CONV_WIDTH, D_MODEL), lambda i: (0, 0)), vec, vec, vec],
        out_specs=pl.BlockSpec((CONV_TILE, D_MODEL), lambda i: (i, 0)),
        out_shape=jax.ShapeDtypeStruct((R_ALL, D_MODEL), BF16),
        scratch_shapes=[pltpu.VMEM((CONV_TILE + 2 * CONV_HALO, D_MODEL), F32),
                        pltpu.VMEM((CONV_TILE, D_MODEL), F32)],
        compiler_params=_cparams(("parallel",)),
        name="cv_conv",
    )(z, z, z, w_dw, b_dw, ln_g, ln_b)


def _sc_conv_kernel(prev_ref, cur_ref, next_ref, bg_ref, w_ref, o_ref, stage_ref):
    i = pl.program_id(0)
    _stage_rows(prev_ref, cur_ref, next_ref, stage_ref, i)

    def emit(r0, lanes, y):
        o_ref[r0:r0 + CONV_ROWS, lanes] = (bg_ref[r0:r0 + CONV_ROWS, lanes].astype(F32) * y).astype(o_ref.dtype)

    _dwconv(stage_ref, w_ref, SHORT_CONV_WIDTH, emit)


def _sc_conv(bg, p, w_conv):
    return pl.pallas_call(
        _sc_conv_kernel,
        grid=(R_ALL // CONV_TILE,),
        in_specs=_halo_specs() + [pl.BlockSpec((CONV_TILE, D_MODEL), lambda i: (i, 0)),
                                  pl.BlockSpec((SHORT_CONV_WIDTH, D_MODEL), lambda i: (0, 0))],
        out_specs=pl.BlockSpec((CONV_TILE, D_MODEL), lambda i: (i, 0)),
        out_shape=jax.ShapeDtypeStruct((R_ALL, D_MODEL), BF16),
        scratch_shapes=[pltpu.VMEM((CONV_TILE + 2 * CONV_HALO, D_MODEL), F32)],
        compiler_params=_cparams(("parallel",)),
        name="sc_conv",
    )(p, p, p, bg, w_conv)


def _rope_tables():
    t = jnp.arange(SEQ)
    quarter = ROPE_AXIS_DIM // 2
    inv_freq = jnp.power(ROPE_BASE, -jnp.arange(quarter, dtype=F32) / quarter)
    ang_r = (t // GRID_W).astype(F32)[:, None] * inv_freq[None, :]
    ang_c = (t % GRID_W).astype(F32)[:, None] * inv_freq[None, :]
    cos = jnp.concatenate([jnp.cos(ang_r), jnp.cos(ang_r), jnp.cos(ang_c), jnp.cos(ang_c)], axis=-1)
    sin = jnp.concatenate([-jnp.sin(ang_r), jnp.sin(ang_r), -jnp.sin(ang_c), jnp.sin(ang_c)], axis=-1)
    cos = jnp.concatenate([cos, jnp.ones((R_CTX, HEAD_DIM), F32)], axis=0)
    sin = jnp.concatenate([sin, jnp.zeros((R_CTX, HEAD_DIM), F32)], axis=0)
    return cos, sin


def kernel(x, c, ctx, c_ctx, w_mod, b_mod, norm_g, ffn_w_in, ffn_w_out, na_w_qkv, na_w_o, na_rpb,
           cv_w_pw1, cv_b_pw1, cv_w_dw, cv_b_dw, cv_ln_g, cv_ln_b, cv_w_pw2, cv_b_pw2,
           sc_w_in, sc_w_conv, sc_w_out, swa_w_qkv, swa_w_o, swa_sink, final_g):
    D = D_MODEL
    xa = jnp.concatenate([x.reshape(R_LAT, D), ctx.reshape(R_CTX, D)], axis=0)
    cc = jnp.concatenate([c, c_ctx[None, :], jnp.zeros((MOD_ROWS - BATCH - 1, D), F32)], axis=0)
    mods = _mods(cc, w_mod, b_mod).reshape(DEPTH, MOD_ROWS, N_MOD, 1, D)
    bf = lambda w: w.astype(BF16)
    row = lambda v: v.reshape(1, -1)

    for i in range(DEPTH):
        kind = i % 4
        last = i == DEPTH - 1
        g = norm_g[i]
        xa = _ffn(xa, mods, i, 0, row(g[0]), bf(ffn_w_in[i, 0]), bf(ffn_w_out[i, 0]), R_ALL)

        if kind == 0:
            qkv = _proj(xa, mods, i, row(g[1]), bf(na_w_qkv[0]), 1024)
            o_lat = _na_attention(qkv, _na_bias_table(na_rpb[0]))
            o_ctx = _ctx_attention(qkv)
            xa = _outproj(o_lat, o_ctx, xa, mods, i, bf(na_w_o[0]), None, R_ALL)
        elif kind == 1:
            z = _proj_glu(xa, mods, i, row(g[1]), bf(cv_w_pw1[0]), row(cv_b_pw1[0]))
            zc = _cv_conv(z, cv_w_dw[0], row(cv_b_dw[0]), row(cv_ln_g[0]), row(cv_ln_b[0]))
            xa = _outproj(zc, None, xa, mods, i, bf(cv_w_pw2[0]), row(cv_b_pw2[0]), R_ALL)
        elif kind == 2:
            bg, p = _proj_sc(xa, mods, i, row(g[1]), bf(sc_w_in[0]))
            y = _sc_conv(bg, p, sc_w_conv[0])
            xa = _outproj(y, None, xa, mods, i, bf(sc_w_out[0]), None, R_ALL)
        else:
            cos_t, sin_t = _rope_tables()
            qkv = _proj_rope(xa, mods, i, row(g[1]), bf(swa_w_qkv[0]), cos_t, sin_t,
                             D + N_KV_HEADS * HEAD_DIM)
            sink_rows = jnp.repeat(swa_sink[0].reshape(N_KV_HEADS, SWA_G), SWA_BLOCK, axis=1)[..., None]
            o_lat = _swa_attention(qkv, sink_rows)
            xa = _outproj(o_lat, None, xa, mods, i, bf(swa_w_o[0]), None, R_LAT)

        rows = R_LAT if last else R_ALL
        xa = _ffn(xa, mods, i, 6, row(g[2]), bf(ffn_w_in[i, 1]), bf(ffn_w_out[i, 1]), rows,
                  final_g=row(final_g) if last else None)

    return xa.reshape(BATCH, SEQ, D)
```

```python
import functools

import jax
import jax.numpy as jnp
from jax import lax
from jax.experimental import pallas as pl
from jax.experimental.pallas import tpu as pltpu

D_MODEL = 2048
BATCH = 4
SEQ = 4096
DEPTH = 4
GRID_W = 64
CTX_LEN = 256
N_HEADS = 16
HEAD_DIM = D_MODEL // N_HEADS
N_KV_HEADS = 4
NA_WIN_R = 8
NA_WIN_C = 16
CONV_WIDTH = 31
SHORT_CONV_WIDTH = 3
SWA_WINDOW = 128
SWA_BLOCK = 128
D_FF = ((8 * D_MODEL // 3 + 255) // 256) * 256
ROPE_BASE = 10000.0
ROPE_AXIS_DIM = HEAD_DIM // 2
N_MOD = 9
MACARON_WEIGHT = 0.5
EPS = 1e-6
NEG_INF = -1e30

R_LAT = BATCH * SEQ
R_CTX = BATCH * CTX_LEN
R_ALL = R_LAT + R_CTX
MOD_ROWS = 8
ATTN_SCALE = HEAD_DIM ** -0.5

F32 = jnp.float32
BF16 = jnp.bfloat16

VMEM_LIMIT_BYTES = 56 * 1024 * 1024

TM = 512
TF = 512
CONV_TILE = 256


def _cparams(sem):
    return pltpu.CompilerParams(dimension_semantics=sem, vmem_limit_bytes=VMEM_LIMIT_BYTES)


def _mod_row(i, tm):
    return jnp.minimum(i // (SEQ // tm), BATCH)


def _mod_spec(layer, k, tm, width=D_MODEL, col=None):
    if col is None:
        return pl.BlockSpec((None, None, None, 1, width),
                            lambda i, j: (layer, _mod_row(i, tm), k, 0, 0))
    return pl.BlockSpec((None, None, None, 1, width),
                        lambda i, j: (layer, _mod_row(i, tm), k, 0, col(j)))


def _rms(x, g):
    return x * lax.rsqrt(jnp.mean(x * x, axis=-1, keepdims=True) + EPS) * g


def _norm_mod(x, g, shift, scale):
    return _rms(x, g) * (1.0 + scale) + shift


def _silu(x):
    return x * jax.nn.sigmoid(x)


def _mods_kernel(cc_ref, w_ref, b_ref, o_ref):
    s = _silu(cc_ref[...]).astype(BF16)
    o_ref[...] = jnp.dot(s, w_ref[...].astype(BF16), preferred_element_type=F32) + b_ref[...]


def _mods(cc, w_mod, b_mod):
    tn = 1024
    n = N_MOD * D_MODEL
    return pl.pallas_call(
        _mods_kernel,
        grid=(DEPTH, n // tn),
        in_specs=[
            pl.BlockSpec((MOD_ROWS, D_MODEL), lambda l, j: (0, 0)),
            pl.BlockSpec((None, D_MODEL, tn), lambda l, j: (l, 0, j)),
            pl.BlockSpec((None, 1, tn), lambda l, j: (l, 0, j)),
        ],
        out_specs=pl.BlockSpec((None, MOD_ROWS, tn), lambda l, j: (l, 0, j)),
        out_shape=jax.ShapeDtypeStruct((DEPTH, MOD_ROWS, n), F32),
        compiler_params=_cparams(("parallel", "parallel")),
        name="mods",
    )(cc, w_mod, b_mod.reshape(DEPTH, 1, n))


def _ffn_kernel(*refs, final):
    if final:
        x_ref, g_ref, sh_ref, sc_ref, gt_ref, wg_ref, wu_ref, wo_ref, fg_ref, o_ref, h_ref = refs
    else:
        x_ref, g_ref, sh_ref, sc_ref, gt_ref, wg_ref, wu_ref, wo_ref, o_ref, h_ref = refs
    f = pl.program_id(1)

    @pl.when(f == 0)
    def _():
        h_ref[...] = _norm_mod(x_ref[...], g_ref[...], sh_ref[...], sc_ref[...]).astype(BF16)
        o_ref[...] = jnp.zeros_like(o_ref)

    h = h_ref[...]
    gate = jnp.dot(h, wg_ref[...], preferred_element_type=F32)
    up = jnp.dot(h, wu_ref[...], preferred_element_type=F32)
    a = (_silu(gate) * up).astype(BF16)
    o_ref[...] += jnp.dot(a, wo_ref[...], preferred_element_type=F32)

    @pl.when(f == pl.num_programs(1) - 1)
    def _():
        y = x_ref[...] + MACARON_WEIGHT * gt_ref[...] * o_ref[...]
        if final:
            y = _rms(y, fg_ref[...])
        o_ref[...] = y


def _ffn(x, mods, layer, half, g, w_in, w_out, rows, final_g=None):
    tm, tf = TM, TF
    nfc = D_FF // tf
    k0 = 6 * half
    final = final_g is not None
    in_specs = [
        pl.BlockSpec((tm, D_MODEL), lambda i, f: (i, 0)),
        pl.BlockSpec((1, D_MODEL), lambda i, f: (0, 0)),
        _mod_spec(layer, k0, tm),
        _mod_spec(layer, k0 + 1, tm),
        _mod_spec(layer, k0 + 2, tm),
        pl.BlockSpec((None, None, D_MODEL, tf), lambda i, f: (layer, half, 0, f)),
        pl.BlockSpec((None, None, D_MODEL, tf), lambda i, f: (layer, half, 0, nfc + f)),
        pl.BlockSpec((None, None, tf, D_MODEL), lambda i, f: (layer, half, f, 0)),
    ]
    args = [x, g, mods, mods, mods, w_in, w_in, w_out]
    if final:
        in_specs.append(pl.BlockSpec((1, D_MODEL), lambda i, f: (0, 0)))
        args.append(final_g)
    return pl.pallas_call(
        functools.partial(_ffn_kernel, final=final),
        grid=(rows // tm, nfc),
        in_specs=in_specs,
        out_specs=pl.BlockSpec((tm, D_MODEL), lambda i, f: (i, 0)),
        out_shape=jax.ShapeDtypeStruct((rows, D_MODEL), F32),
        scratch_shapes=[pltpu.VMEM((tm, D_MODEL), BF16)],
        compiler_params=_cparams(("parallel", "arbitrary")),
        name="ffn_final" if final else "ffn",
    )(*args)


def _proj_prologue(x_ref, g_ref, sh_ref, sc_ref, h_ref):
    @pl.when(pl.program_id(1) == 0)
    def _():
        h_ref[...] = _norm_mod(x_ref[...], g_ref[...], sh_ref[...], sc_ref[...]).astype(BF16)


def _proj_kernel(x_ref, g_ref, sh_ref, sc_ref, w_ref, o_ref, h_ref):
    _proj_prologue(x_ref, g_ref, sh_ref, sc_ref, h_ref)
    o_ref[...] = jnp.dot(h_ref[...], w_ref[...], preferred_element_type=F32).astype(o_ref.dtype)


def _rope_swap(y):
    n = y.shape[-1]
    lane = lax.broadcasted_iota(jnp.int32, y.shape, 1)
    return jnp.where(lane % ROPE_AXIS_DIM < ROPE_AXIS_DIM // 2,
                     pltpu.roll(y, n - ROPE_AXIS_DIM // 2, 1),
                     pltpu.roll(y, ROPE_AXIS_DIM // 2, 1))


def _proj_rope_kernel(x_ref, g_ref, sh_ref, sc_ref, w_ref, cos_ref, sin_ref, o_ref, h_ref, *, n_rope):
    _proj_prologue(x_ref, g_ref, sh_ref, sc_ref, h_ref)
    j = pl.program_id(1)
    y = jnp.dot(h_ref[...], w_ref[...], preferred_element_type=F32)

    @pl.when(j < n_rope)
    def _():
        cos = cos_ref[...]
        sin = sin_ref[...]
        for hh in range(y.shape[-1] // HEAD_DIM):
            yh = y[:, hh * HEAD_DIM:(hh + 1) * HEAD_DIM]
            o_ref[:, hh * HEAD_DIM:(hh + 1) * HEAD_DIM] = (yh * cos + _rope_swap(yh) * sin).astype(o_ref.dtype)

    @pl.when(j >= n_rope)
    def _():
        o_ref[...] = y.astype(o_ref.dtype)


def _proj_common_specs(layer, tm):
    return [
        pl.BlockSpec((tm, D_MODEL), lambda i, j: (i, 0)),
        pl.BlockSpec((1, D_MODEL), lambda i, j: (0, 0)),
        _mod_spec(layer, 3, tm),
        _mod_spec(layer, 4, tm),
    ]


def _proj(x, mods, layer, g, w, tn):
    tm = TM
    n = w.shape[1]
    return pl.pallas_call(
        _proj_kernel,
        grid=(R_ALL // tm, n // tn),
        in_specs=_proj_common_specs(layer, tm) + [pl.BlockSpec((D_MODEL, tn), lambda i, j: (0, j))],
        out_specs=pl.BlockSpec((tm, tn), lambda i, j: (i, j)),
        out_shape=jax.ShapeDtypeStruct((R_ALL, n), BF16),
        scratch_shapes=[pltpu.VMEM((tm, D_MODEL), BF16)],
        compiler_params=_cparams(("parallel", "arbitrary")),
        name="proj",
    )(x, g, mods, mods, w)


def _proj_rope(x, mods, layer, g, w, cos_t, sin_t, n_rope_cols):
    tm, tn = TM, 512
    n = w.shape[1]
    n_lat = R_LAT // tm
    tpb = SEQ // tm

    def tab(i, j):
        return (jnp.where(i < n_lat, i % tpb, tpb + i - n_lat), 0)

    return pl.pallas_call(
        functools.partial(_proj_rope_kernel, n_rope=n_rope_cols // tn),
        grid=(R_ALL // tm, n // tn),
        in_specs=_proj_common_specs(layer, tm) + [
            pl.BlockSpec((D_MODEL, tn), lambda i, j: (0, j)),
            pl.BlockSpec((tm, HEAD_DIM), tab),
            pl.BlockSpec((tm, HEAD_DIM), tab),
        ],
        out_specs=pl.BlockSpec((tm, tn), lambda i, j: (i, j)),
        out_shape=jax.ShapeDtypeStruct((R_ALL, n), BF16),
        scratch_shapes=[pltpu.VMEM((tm, D_MODEL), BF16)],
        compiler_params=_cparams(("parallel", "arbitrary")),
        name="proj_rope",
    )(x, g, mods, mods, w, cos_t, sin_t)


def _proj_glu_kernel(x_ref, g_ref, sh_ref, sc_ref, wa_ref, wg_ref, ba_ref, bg_ref, o_ref, h_ref):
    _proj_prologue(x_ref, g_ref, sh_ref, sc_ref, h_ref)
    h = h_ref[...]
    a = jnp.dot(h, wa_ref[...], preferred_element_type=F32) + ba_ref[...]
    gt = jnp.dot(h, wg_ref[...], preferred_element_type=F32) + bg_ref[...]
    o_ref[...] = a * jax.nn.sigmoid(gt)


def _proj_glu(x, mods, layer, g, w, b):
    tm, tn = TM, 512
    nc = D_MODEL // tn
    return pl.pallas_call(
        _proj_glu_kernel,
        grid=(R_ALL // tm, nc),
        in_specs=_proj_common_specs(layer, tm) + [
            pl.BlockSpec((D_MODEL, tn), lambda i, j: (0, j)),
            pl.BlockSpec((D_MODEL, tn), lambda i, j: (0, nc + j)),
            pl.BlockSpec((1, tn), lambda i, j: (0, j)),
            pl.BlockSpec((1, tn), lambda i, j: (0, nc + j)),
        ],
        out_specs=pl.BlockSpec((tm, tn), lambda i, j: (i, j)),
        out_shape=jax.ShapeDtypeStruct((R_ALL, D_MODEL), F32),
        scratch_shapes=[pltpu.VMEM((tm, D_MODEL), BF16)],
        compiler_params=_cparams(("parallel", "arbitrary")),
        name="proj_glu",
    )(x, g, mods, mods, w, w, b, b)


def _proj_sc_kernel(x_ref, g_ref, sh_ref, sc_ref, wb_ref, wc_ref, wx_ref, bg_ref, p_ref, h_ref):
    _proj_prologue(x_ref, g_ref, sh_ref, sc_ref, h_ref)
    h = h_ref[...]
    bg_ref[...] = jnp.dot(h, wb_ref[...], preferred_element_type=F32).astype(bg_ref.dtype)
    cg = jnp.dot(h, wc_ref[...], preferred_element_type=F32)
    xin = jnp.dot(h, wx_ref[...], preferred_element_type=F32)
    p_ref[...] = cg * xin


def _proj_sc(x, mods, layer, g, w):
    tm, tn = TM, 512
    nc = D_MODEL // tn
    return pl.pallas_call(
        _proj_sc_kernel,
        grid=(R_ALL // tm, nc),
        in_specs=_proj_common_specs(layer, tm) + [
            pl.BlockSpec((D_MODEL, tn), lambda i, j: (0, j)),
            pl.BlockSpec((D_MODEL, tn), lambda i, j: (0, nc + j)),
            pl.BlockSpec((D_MODEL, tn), lambda i, j: (0, 2 * nc + j)),
        ],
        out_specs=[pl.BlockSpec((tm, tn), lambda i, j: (i, j)),
                   pl.BlockSpec((tm, tn), lambda i, j: (i, j))],
        out_shape=[jax.ShapeDtypeStruct((R_ALL, D_MODEL), BF16),
                   jax.ShapeDtypeStruct((R_ALL, D_MODEL), F32)],
        scratch_shapes=[pltpu.VMEM((tm, D_MODEL), BF16)],
        compiler_params=_cparams(("parallel", "arbitrary")),
        name="proj_sc",
    )(x, g, mods, mods, w, w, w)


def _outproj_kernel(*refs, n_lat, has_ctx, has_bias):
    refs = list(refs)
    ol_ref = refs.pop(0)
    oc_ref = refs.pop(0) if has_ctx else None
    w_ref = refs.pop(0)
    b_ref = refs.pop(0) if has_bias else None
    x_ref, gt_ref, o_ref = refs
    i = pl.program_id(0)

    def finish(src_ref):
        y = jnp.dot(src_ref[...], w_ref[...], preferred_element_type=F32)
        if has_bias:
            y = y + b_ref[...]
        o_ref[...] = x_ref[...] + gt_ref[...] * y

    if has_ctx:
        @pl.when(i < n_lat)
        def _():
            finish(ol_ref)

        @pl.when(i >= n_lat)
        def _():
            finish(oc_ref)
    else:
        finish(ol_ref)


def _outproj(o_lat, o_ctx, x, mods, layer, w, bias, rows):
    tm, tn = TM, D_MODEL
    n_lat = R_LAT // tm
    has_ctx = o_ctx is not None
    has_bias = bias is not None
    in_specs = []
    args = []
    if has_ctx:
        in_specs.append(pl.BlockSpec((tm, D_MODEL), lambda i, j: (jnp.minimum(i, n_lat - 1), 0)))
        in_specs.append(pl.BlockSpec((tm, D_MODEL), lambda i, j: (jnp.maximum(i - n_lat, 0), 0)))
        args += [o_lat, o_ctx]
    else:
        in_specs.append(pl.BlockSpec((tm, D_MODEL), lambda i, j: (i, 0)))
        args.append(o_lat)
    in_specs.append(pl.BlockSpec((D_MODEL, tn), lambda i, j: (0, j)))
    args.append(w)
    if has_bias:
        in_specs.append(pl.BlockSpec((1, tn), lambda i, j: (0, j)))
        args.append(bias)
    in_specs.append(pl.BlockSpec((tm, tn), lambda i, j: (i, j)))
    in_specs.append(_mod_spec(layer, 5, tm, width=tn, col=lambda j: j))
    args += [x, mods]
    return pl.pallas_call(
        functools.partial(_outproj_kernel, n_lat=n_lat, has_ctx=has_ctx, has_bias=has_bias),
        grid=(rows // tm, D_MODEL // tn),
        in_specs=in_specs,
        out_specs=pl.BlockSpec((tm, tn), lambda i, j: (i, j)),
        out_shape=jax.ShapeDtypeStruct((rows, D_MODEL), F32),
        compiler_params=_cparams(("parallel", "parallel")),
        name="outproj",
    )(*args)


GRID_ROWS = SEQ // GRID_W
NA_RB = 4
NA_KROWS = 12
NA_Q = NA_RB * GRID_W
NA_KEYS = NA_KROWS * GRID_W
NA_OFFS = (0, NA_RB, 2 * NA_RB)
NA_RPB_R = 2 * NA_WIN_R - 1
NA_RPB_C = 2 * NA_WIN_C - 1


def _na_key_row0(r0):
    return jnp.clip(r0 - NA_WIN_R // 2, 0, GRID_ROWS - NA_KROWS)


def _na_bias_table(rpb):
    lanes = 2 * GRID_W
    period = jnp.concatenate([rpb[..., NA_WIN_C - 1:],
                              jnp.zeros(rpb.shape[:2] + (lanes - NA_RPB_C,), rpb.dtype),
                              rpb[..., :NA_WIN_C - 1]], axis=-1)
    flat = jnp.tile(period, (1, 1, GRID_W))[..., :GRID_W * (lanes - 1)]
    toep = flat.reshape(N_HEADS, NA_RPB_R, GRID_W, lanes - 1)[..., :GRID_W]

    offs = jnp.asarray(NA_OFFS)[:, None, None]
    a = jnp.arange(NA_RB)[None, :, None]
    jr = jnp.arange(NA_KROWS)[None, None, :]
    rs_rel = jnp.stack([jnp.zeros((NA_RB, 1), jnp.int32),
                        jnp.arange(NA_RB)[:, None],
                        jnp.full((NA_RB, 1), NA_KROWS - NA_WIN_R, jnp.int32)])
    row_ok = (jr >= rs_rel) & (jr < rs_rel + NA_WIN_R)
    row_off = jnp.clip(jr - offs - a + NA_WIN_R - 1, 0, NA_RPB_R - 1)
    c = jnp.arange(GRID_W)[:, None]
    kc = jnp.arange(GRID_W)[None, :]
    cs = jnp.clip(c - NA_WIN_C // 2, 0, GRID_W - NA_WIN_C)
    col_ok = (kc >= cs) & (kc < cs + NA_WIN_C)
    tab = jnp.take(toep, row_off.reshape(-1), axis=1)
    tab = tab.reshape(N_HEADS, len(NA_OFFS), NA_RB, NA_KROWS, GRID_W, GRID_W)
    ok = row_ok[None, :, :, :, None, None] & col_ok[None, None, None, None, :, :]
    tab = jnp.where(ok, tab, NEG_INF)
    return tab.transpose(1, 0, 2, 4, 3, 5).reshape(len(NA_OFFS), N_HEADS, NA_Q, NA_KEYS).astype(F32)


_NT = (((1,), (1,)), ((), ()))


def _na_kernel(q_ref, k_ref, v_ref, kc_ref, vc_ref, bias_ref, o_ref):
    kc = kc_ref[...]
    vc = vc_ref[...]

    def body(t, carry):
        r0 = t * NA_RB
        ks = _na_key_row0(r0)
        q0 = pl.multiple_of(r0 * GRID_W, NA_Q)
        k0 = pl.multiple_of(ks * GRID_W, GRID_W)
        q = q_ref[pl.ds(q0, NA_Q), :]
        kw = k_ref[pl.ds(k0, NA_KEYS), :]
        vw = v_ref[pl.ds(k0, NA_KEYS), :]
        bias = bias_ref[(r0 - ks) // NA_RB]
        s_loc = lax.dot_general(q, kw, _NT, preferred_element_type=F32) * ATTN_SCALE + bias
        s_ctx = lax.dot_general(q, kc, _NT, preferred_element_type=F32) * ATTN_SCALE
        m = jnp.maximum(jnp.max(s_loc, axis=-1, keepdims=True), jnp.max(s_ctx, axis=-1, keepdims=True))
        p_loc = jnp.exp(s_loc - m)
        p_ctx = jnp.exp(s_ctx - m)
        l = jnp.sum(p_loc, axis=-1, keepdims=True) + jnp.sum(p_ctx, axis=-1, keepdims=True)
        o = (jnp.dot(p_loc.astype(BF16), vw, preferred_element_type=F32)
             + jnp.dot(p_ctx.astype(BF16), vc, preferred_element_type=F32))
        o_ref[pl.ds(q0, NA_Q), :] = (o / l).astype(o_ref.dtype)
        return carry

    lax.fori_loop(0, GRID_ROWS // NA_RB, body, 0, unroll=2)


def _na_attention(qkv, bias_tab):
    h3 = N_HEADS
    ctx_blk = R_LAT // CTX_LEN
    return pl.pallas_call(
        _na_kernel,
        grid=(BATCH, N_HEADS),
        in_specs=[
            pl.BlockSpec((SEQ, HEAD_DIM), lambda b, h: (b, h)),
            pl.BlockSpec((SEQ, HEAD_DIM), lambda b, h: (b, h3 + h)),
            pl.BlockSpec((SEQ, HEAD_DIM), lambda b, h: (b, 2 * h3 + h)),
            pl.BlockSpec((CTX_LEN, HEAD_DIM), lambda b, h: (ctx_blk + b, h3 + h)),
            pl.BlockSpec((CTX_LEN, HEAD_DIM), lambda b, h: (ctx_blk + b, 2 * h3 + h)),
            pl.BlockSpec((len(NA_OFFS), None, NA_Q, NA_KEYS), lambda b, h: (0, h, 0, 0)),
        ],
        out_specs=pl.BlockSpec((SEQ, HEAD_DIM), lambda b, h: (b, h)),
        out_shape=jax.ShapeDtypeStruct((R_LAT, D_MODEL), BF16),
        compiler_params=_cparams(("parallel", "parallel")),
        name="na_attn",
    )(qkv, qkv, qkv, qkv, qkv, bias_tab)


def _ctx_attn_kernel(q_ref, k_ref, v_ref, o_ref):
    s = lax.dot_general(q_ref[...], k_ref[...], _NT, preferred_element_type=F32) * ATTN_SCALE
    p = jnp.exp(s - jnp.max(s, axis=-1, keepdims=True))
    l = jnp.sum(p, axis=-1, keepdims=True)
    o = jnp.dot(p.astype(BF16), v_ref[...], preferred_element_type=F32)
    o_ref[...] = (o / l).astype(o_ref.dtype)


def _ctx_attention(qkv):
    h3 = N_HEADS
    ctx_blk = R_LAT // CTX_LEN
    return pl.pallas_call(
        _ctx_attn_kernel,
        grid=(BATCH, N_HEADS),
        in_specs=[
            pl.BlockSpec((CTX_LEN, HEAD_DIM), lambda b, h: (ctx_blk + b, h)),
            pl.BlockSpec((CTX_LEN, HEAD_DIM), lambda b, h: (ctx_blk + b, h3 + h)),
            pl.BlockSpec((CTX_LEN, HEAD_DIM), lambda b, h: (ctx_blk + b, 2 * h3 + h)),
        ],
        out_specs=pl.BlockSpec((CTX_LEN, HEAD_DIM), lambda b, h: (b, h)),
        out_shape=jax.ShapeDtypeStruct((R_CTX, D_MODEL), BF16),
        compiler_params=_cparams(("parallel", "parallel")),
        name="ctx_attn",
    )(qkv, qkv, qkv)


SWA_G = N_HEADS // N_KV_HEADS
SWA_KEYS = 3 * SWA_BLOCK


SWA_NQ = 4


def _swa_kernel(q_ref, k_ref, v_ref, kc_ref, vc_ref, sink_ref, o_ref):
    kc = kc_ref[...]
    vc = vc_ref[...]
    sink = sink_ref[...]
    shape = (SWA_G * SWA_BLOCK, SWA_KEYS)
    delta = lax.broadcasted_iota(jnp.int32, shape, 1) - lax.broadcasted_iota(jnp.int32, shape, 0) % SWA_BLOCK
    for qi in range(SWA_NQ):
        n = pl.program_id(2) * SWA_NQ + qi
        ws = pl.multiple_of(jnp.clip((n - 1) * SWA_BLOCK, 0, SEQ - SWA_KEYS), SWA_BLOCK)
        rows = slice(qi * SWA_BLOCK, (qi + 1) * SWA_BLOCK)
        q = jnp.concatenate([q_ref[rows, g * HEAD_DIM:(g + 1) * HEAD_DIM] for g in range(SWA_G)], axis=0)
        kw = k_ref[pl.ds(ws, SWA_KEYS), :]
        vw = v_ref[pl.ds(ws, SWA_KEYS), :]
        s_loc = lax.dot_general(q, kw, _NT, preferred_element_type=F32) * ATTN_SCALE
        s_loc = jnp.where(jnp.abs(delta + (ws - n * SWA_BLOCK)) <= SWA_WINDOW, s_loc, NEG_INF)
        s_ctx = lax.dot_general(q, kc, _NT, preferred_element_type=F32) * ATTN_SCALE
        m = jnp.maximum(jnp.maximum(jnp.max(s_loc, axis=-1, keepdims=True),
                                    jnp.max(s_ctx, axis=-1, keepdims=True)), sink)
        p_loc = jnp.exp(s_loc - m)
        p_ctx = jnp.exp(s_ctx - m)
        l = (jnp.sum(p_loc, axis=-1, keepdims=True) + jnp.sum(p_ctx, axis=-1, keepdims=True)
             + jnp.exp(sink - m))
        o = (jnp.dot(p_loc.astype(BF16), vw, preferred_element_type=F32)
             + jnp.dot(p_ctx.astype(BF16), vc, preferred_element_type=F32)) / l
        for g in range(SWA_G):
            o_ref[rows, g * HEAD_DIM:(g + 1) * HEAD_DIM] = o[g * SWA_BLOCK:(g + 1) * SWA_BLOCK].astype(o_ref.dtype)


def _swa_attention(qkv, sink_rows):
    nb = SEQ // (SWA_BLOCK * SWA_NQ)
    kcol = N_HEADS
    vcol = N_HEADS + N_KV_HEADS
    ctx_blk = R_LAT // CTX_LEN
    return pl.pallas_call(
        _swa_kernel,
        grid=(BATCH, N_KV_HEADS, nb),
        in_specs=[
            pl.BlockSpec((SWA_NQ * SWA_BLOCK, SWA_G * HEAD_DIM), lambda b, kv, n: (b * nb + n, kv)),
            pl.BlockSpec((SEQ, HEAD_DIM), lambda b, kv, n: (b, kcol + kv)),
            pl.BlockSpec((SEQ, HEAD_DIM), lambda b, kv, n: (b, vcol + kv)),
            pl.BlockSpec((CTX_LEN, HEAD_DIM), lambda b, kv, n: (ctx_blk + b, kcol + kv)),
            pl.BlockSpec((CTX_LEN, HEAD_DIM), lambda b, kv, n: (ctx_blk + b, vcol + kv)),
            pl.BlockSpec((None, SWA_G * SWA_BLOCK, 1), lambda b, kv, n: (kv, 0, 0)),
        ],
        out_specs=pl.BlockSpec((SWA_NQ * SWA_BLOCK, SWA_G * HEAD_DIM), lambda b, kv, n: (b * nb + n, kv)),
        out_shape=jax.ShapeDtypeStruct((R_LAT, D_MODEL), BF16),
        compiler_params=_cparams(("parallel", "parallel", "arbitrary")),
        name="swa_attn",
    )(qkv, qkv, qkv, qkv, qkv, sink_rows)


CONV_HALO = 16
CONV_LANES = 128
CONV_ROWS = 64


def _halo_flags(i):
    tiles_per_seq = SEQ // CONV_TILE
    latent = i < R_LAT // CONV_TILE
    has_prev = jnp.logical_and(latent, i % tiles_per_seq != 0)
    has_next = jnp.logical_and(latent, i % tiles_per_seq != tiles_per_seq - 1)
    return has_prev, has_next


def _stage_rows(prev_ref, cur_ref, next_ref, stage_ref, i):
    has_prev, has_next = _halo_flags(i)
    stage_ref[0:CONV_HALO, :] = jnp.where(has_prev, prev_ref[...], 0.0)
    stage_ref[CONV_HALO:CONV_HALO + CONV_TILE, :] = cur_ref[...]
    stage_ref[CONV_HALO + CONV_TILE:, :] = jnp.where(has_next, next_ref[...], 0.0)


def _dwconv(stage_ref, w_ref, width, emit):
    pad = width // 2

    def lane_chunk(ci, carry):
        lanes = pl.ds(pl.multiple_of(ci * CONV_LANES, CONV_LANES), CONV_LANES)
        w = w_ref[:, lanes]
        for r0 in range(0, CONV_TILE, CONV_ROWS):
            acc = None
            for k in range(width):
                start = CONV_HALO + r0 + k - pad
                term = stage_ref[start:start + CONV_ROWS, lanes] * w[k:k + 1, :]
                acc = term if acc is None else acc + term
            emit(r0, lanes, acc)
        return carry

    lax.fori_loop(0, D_MODEL // CONV_LANES, lane_chunk, 0)


def _cv_conv_kernel(prev_ref, cur_ref, next_ref, w_ref, b_ref, lg_ref, lb_ref, o_ref, stage_ref, y_ref):
    i = pl.program_id(0)
    _stage_rows(prev_ref, cur_ref, next_ref, stage_ref, i)

    def emit(r0, lanes, y):
        y_ref[r0:r0 + CONV_ROWS, lanes] = y + b_ref[:, lanes]

    _dwconv(stage_ref, w_ref, CONV_WIDTH, emit)
    y = y_ref[...]
    mu = jnp.mean(y, axis=-1, keepdims=True)
    yc = y - mu
    var = jnp.mean(yc * yc, axis=-1, keepdims=True)
    z = yc * lax.rsqrt(var + EPS) * lg_ref[...] + lb_ref[...]
    o_ref[...] = _silu(z).astype(o_ref.dtype)


def _halo_specs():
    per = CONV_TILE // CONV_HALO
    last = R_ALL // CONV_HALO - 1
    return [
        pl.BlockSpec((CONV_HALO, D_MODEL), lambda i: (jnp.maximum(i * per - 1, 0), 0)),
        pl.BlockSpec((CONV_TILE, D_MODEL), lambda i: (i, 0)),
        pl.BlockSpec((CONV_HALO, D_MODEL), lambda i: (jnp.minimum((i + 1) * per, last), 0)),
    ]


def _cv_conv(z, w_dw, b_dw, ln_g, ln_b):
    vec = pl.BlockSpec((1, D_MODEL), lambda i: (0, 0))
    return pl.pallas_call(
        _cv_conv_kernel,
        grid=(R_ALL // CONV_TILE,),
        in_specs=_halo_specs() + [pl.BlockSpec((CONV_WIDTH, D_MODEL), lambda i: (0, 0)), vec, vec, vec],
        out_specs=pl.BlockSpec((CONV_TILE, D_MODEL), lambda i: (i, 0)),
        out_shape=jax.ShapeDtypeStruct((R_ALL, D_MODEL), BF16),
        scratch_shapes=[pltpu.VMEM((CONV_TILE + 2 * CONV_HALO, D_MODEL), F32),
                        pltpu.VMEM((CONV_TILE, D_MODEL), F32)],
        compiler_params=_cparams(("parallel",)),
        name="cv_conv",
    )(z, z, z, w_dw, b_dw, ln_g, ln_b)


def _sc_conv_kernel(prev_ref, cur_ref, next_ref, bg_ref, w_ref, o_ref, stage_ref):
    i = pl.program_id(0)
    _stage_rows(prev_ref, cur_ref, next_ref, stage_ref, i)

    def emit(r0, lanes, y):
        o_ref[r0:r0 + CONV_ROWS, lanes] = (bg_ref[r0:r0 + CONV_ROWS, lanes].astype(F32) * y).astype(o_ref.dtype)

    _dwconv(stage_ref, w_ref, SHORT_CONV_WIDTH, emit)


def _sc_conv(bg, p, w_conv):
    return pl.pallas_call(
        _sc_conv_kernel,
        grid=(R_ALL // CONV_TILE,),
        in_specs=_halo_specs() + [pl.BlockSpec((CONV_TILE, D_MODEL), lambda i: (i, 0)),
                                  pl.BlockSpec((SHORT_CONV_WIDTH, D_MODEL), lambda i: (0, 0))],
        out_specs=pl.BlockSpec((CONV_TILE, D_MODEL), lambda i: (i, 0)),
        out_shape=jax.ShapeDtypeStruct((R_ALL, D_MODEL), BF16),
        scratch_shapes=[pltpu.VMEM((CONV_TILE + 2 * CONV_HALO, D_MODEL), F32)],
        compiler_params=_cparams(("parallel",)),
        name="sc_conv",
    )(p, p, p, bg, w_conv)


def _rope_tables():
    t = jnp.arange(SEQ)
    quarter = ROPE_AXIS_DIM // 2
    inv_freq = jnp.power(ROPE_BASE, -jnp.arange(quarter, dtype=F32) / quarter)
    ang_r = (t // GRID_W).astype(F32)[:, None] * inv_freq[None, :]
    ang_c = (t % GRID_W).astype(F32)[:, None] * inv_freq[None, :]
    cos = jnp.concatenate([jnp.cos(ang_r), jnp.cos(ang_r), jnp.cos(ang_c), jnp.cos(ang_c)], axis=-1)
    sin = jnp.concatenate([-jnp.sin(ang_r), jnp.sin(ang_r), -jnp.sin(ang_c), jnp.sin(ang_c)], axis=-1)
    cos = jnp.concatenate([cos, jnp.ones((R_CTX, HEAD_DIM), F32)], axis=0)
    sin = jnp.concatenate([sin, jnp.zeros((R_CTX, HEAD_DIM), F32)], axis=0)
    return cos, sin


def kernel(x, c, ctx, c_ctx, w_mod, b_mod, norm_g, ffn_w_in, ffn_w_out, na_w_qkv, na_w_o, na_rpb,
           cv_w_pw1, cv_b_pw1, cv_w_dw, cv_b_dw, cv_ln_g, cv_ln_b, cv_w_pw2, cv_b_pw2,
           sc_w_in, sc_w_conv, sc_w_out, swa_w_qkv, swa_w_o, swa_sink, final_g):
    D = D_MODEL
    xa = jnp.concatenate([x.reshape(R_LAT, D), ctx.reshape(R_CTX, D)], axis=0)
    cc = jnp.concatenate([c, c_ctx[None, :], jnp.zeros((MOD_ROWS - BATCH - 1, D), F32)], axis=0)
    mods = _mods(cc, w_mod, b_mod).reshape(DEPTH, MOD_ROWS, N_MOD, 1, D)
    bf = lambda w: w.astype(BF16)
    row = lambda v: v.reshape(1, -1)
    w_in = bf(ffn_w_in)
    w_out = bf(ffn_w_out)

    for i in range(DEPTH):
        kind = i % 4
        last = i == DEPTH - 1
        g = norm_g[i]
        xa = _ffn(xa, mods, i, 0, row(g[0]), w_in, w_out, R_ALL)

        if kind == 0:
            qkv = _proj(xa, mods, i, row(g[1]), bf(na_w_qkv[0]), 1024)
            o_lat = _na_attention(qkv, _na_bias_table(na_rpb[0]))
            o_ctx = _ctx_attention(qkv)
            xa = _outproj(o_lat, o_ctx, xa, mods, i, bf(na_w_o[0]), None, R_ALL)
        elif kind == 1:
            z = _proj_glu(xa, mods, i, row(g[1]), bf(cv_w_pw1[0]), row(cv_b_pw1[0]))
            zc = _cv_conv(z, cv_w_dw[0], row(cv_b_dw[0]), row(cv_ln_g[0]), row(cv_ln_b[0]))
            xa = _outproj(zc, None, xa, mods, i, bf(cv_w_pw2[0]), row(cv_b_pw2[0]), R_ALL)
        elif kind == 2:
            bg, p = _proj_sc(xa, mods, i, row(g[1]), bf(sc_w_in[0]))
            y = _sc_conv(bg, p, sc_w_conv[0])
            xa = _outproj(y, None, xa, mods, i, bf(sc_w_out[0]), None, R_ALL)
        else:
            cos_t, sin_t = _rope_tables()
            qkv = _proj_rope(xa, mods, i, row(g[1]), bf(swa_w_qkv[0]), cos_t, sin_t,
                             D + N_KV_HEADS * HEAD_DIM)
            sink_rows = jnp.repeat(swa_sink[0].reshape(N_KV_HEADS, SWA_G), SWA_BLOCK, axis=1)[..., None]
            o_lat = _swa_attention(qkv, sink_rows)
            xa = _outproj(o_lat, None, xa, mods, i, bf(swa_w_o[0]), None, R_LAT)

        rows = R_LAT if last else R_ALL
        xa = _ffn(xa, mods, i, 1, row(g[2]), w_in, w_out, rows, final_g=row(final_g) if last else None)

    return xa.reshape(BATCH, SEQ, D)
```

```python
import functools

import jax
import jax.numpy as jnp
from jax import lax
from jax.experimental import pallas as pl
from jax.experimental.pallas import tpu as pltpu

D_MODEL = 2048
BATCH = 4
SEQ = 4096
DEPTH = 4
GRID_W = 64
CTX_LEN = 256
N_HEADS = 16
HEAD_DIM = D_MODEL // N_HEADS
N_KV_HEADS = 4
NA_WIN_R = 8
NA_WIN_C = 16
CONV_WIDTH = 31
SHORT_CONV_WIDTH = 3
SWA_WINDOW = 128
SWA_BLOCK = 128
D_FF = ((8 * D_MODEL // 3 + 255) // 256) * 256
ROPE_BASE = 10000.0
ROPE_AXIS_DIM = HEAD_DIM // 2
N_MOD = 9
MACARON_WEIGHT = 0.5
EPS = 1e-6
NEG_INF = -1e30

R_LAT = BATCH * SEQ
R_CTX = BATCH * CTX_LEN
R_ALL = R_LAT + R_CTX
MOD_ROWS = 8
ATTN_SCALE = HEAD_DIM ** -0.5

F32 = jnp.float32
BF16 = jnp.bfloat16

VMEM_LIMIT_BYTES = 56 * 1024 * 1024

TM = 512
TM_PROJ = 1024
TF = 512
CONV_TILE = 256


def _cparams(sem):
    return pltpu.CompilerParams(dimension_semantics=sem, vmem_limit_bytes=VMEM_LIMIT_BYTES)


def _mod_row(i, tm):
    return jnp.minimum(i // (SEQ // tm), BATCH)


def _mod_spec(layer, k, tm, tile=lambda i: i):
    return pl.BlockSpec((None, None, None, 1, D_MODEL),
                        lambda i, j: (layer, _mod_row(tile(i), tm), k, 0, 0))


def _pro_tile(n_tiles):
    return lambda i: jnp.minimum(i, n_tiles - 1)


def _mm_tile(i):
    return jnp.maximum(i - 1, 0)


def _mm_step(i, s):
    return jnp.where(i == 0, 0, s)


def _norm_chunk(x_ref, g_ref, sh_ref, sc_ref, h_ref, n_chunks):
    rows = x_ref.shape[0] // n_chunks
    r = pl.multiple_of(jnp.minimum(pl.program_id(1), n_chunks - 1) * rows, rows)
    h = _norm_mod(x_ref[pl.ds(r, rows), :], g_ref[...], sh_ref[...], sc_ref[...])
    h_ref[pl.program_id(0) % 2, pl.ds(r, rows), :] = h.astype(BF16)


def _rms(x, g):
    return x * lax.rsqrt(jnp.mean(x * x, axis=-1, keepdims=True) + EPS) * g


def _norm_mod(x, g, shift, scale):
    return _rms(x, g) * (1.0 + scale) + shift


def _silu(x):
    return x * jax.nn.sigmoid(x)


def _mods_kernel(cc_ref, w_ref, b_ref, o_ref):
    s = _silu(cc_ref[...]).astype(BF16)
    o_ref[...] = jnp.dot(s, w_ref[...].astype(BF16), preferred_element_type=F32) + b_ref[...]


def _mods(cc, w_mod, b_mod):
    tn = 1024
    n = N_MOD * D_MODEL
    return pl.pallas_call(
        _mods_kernel,
        grid=(DEPTH, n // tn),
        in_specs=[
            pl.BlockSpec((MOD_ROWS, D_MODEL), lambda l, j: (0, 0)),
            pl.BlockSpec((None, D_MODEL, tn), lambda l, j: (l, 0, j)),
            pl.BlockSpec((None, 1, tn), lambda l, j: (l, 0, j)),
        ],
        out_specs=pl.BlockSpec((None, MOD_ROWS, tn), lambda l, j: (l, 0, j)),
        out_shape=jax.ShapeDtypeStruct((DEPTH, MOD_ROWS, n), F32),
        compiler_params=_cparams(("parallel", "parallel")),
        name="mods",
    )(cc, w_mod, b_mod.reshape(DEPTH, 1, n))


FFN_CHUNKS = 8


def _ffn_kernel(*refs, final):
    if final:
        xp_ref, g_ref, sh_ref, sc_ref, xr_ref, gt_ref, wg_ref, wu_ref, wo_ref, fg_ref, o_ref, h_ref = refs
    else:
        xp_ref, g_ref, sh_ref, sc_ref, xr_ref, gt_ref, wg_ref, wu_ref, wo_ref, o_ref, h_ref = refs
    i = pl.program_id(0)
    f = pl.program_id(1)

    def prologue_chunk():
        _norm_chunk(xp_ref, g_ref, sh_ref, sc_ref, h_ref, FFN_CHUNKS)

    @pl.when(i == 0)
    def _():
        prologue_chunk()

    @pl.when(i > 0)
    def _():
        @pl.when(f == 0)
        def _():
            o_ref[...] = jnp.zeros_like(o_ref)

        h = h_ref[(i + 1) % 2]
        gate = jnp.dot(h, wg_ref[...], preferred_element_type=F32)
        up = jnp.dot(h, wu_ref[...], preferred_element_type=F32)
        a = (_silu(gate) * up).astype(BF16)
        o_ref[...] += jnp.dot(a, wo_ref[...], preferred_element_type=F32)
        prologue_chunk()

        @pl.when(f == pl.num_programs(1) - 1)
        def _():
            y = xr_ref[...] + MACARON_WEIGHT * gt_ref[...] * o_ref[...]
            if final:
                y = _rms(y, fg_ref[...])
            o_ref[...] = y


def _ffn(x, mods, layer, half, g, w_in, w_out, rows, final_g=None):
    tm, tf = TM, TF
    nfc = D_FF // tf
    nt = rows // tm
    k0 = 6 * half
    final = final_g is not None
    pro = _pro_tile(nt)
    in_specs = [
        pl.BlockSpec((tm, D_MODEL), lambda i, f: (pro(i), 0)),
        pl.BlockSpec((1, D_MODEL), lambda i, f: (0, 0)),
        _mod_spec(layer, k0, tm, pro),
        _mod_spec(layer, k0 + 1, tm, pro),
        pl.BlockSpec((tm, D_MODEL), lambda i, f: (_mm_tile(i), 0)),
        _mod_spec(layer, k0 + 2, tm, _mm_tile),
        pl.BlockSpec((None, None, D_MODEL, tf), lambda i, f: (layer, half, 0, _mm_step(i, f))),
        pl.BlockSpec((None, None, D_MODEL, tf), lambda i, f: (layer, half, 0, nfc + _mm_step(i, f))),
        pl.BlockSpec((None, None, tf, D_MODEL), lambda i, f: (layer, half, _mm_step(i, f), 0)),
    ]
    args = [x, g, mods, mods, x, mods, w_in, w_in, w_out]
    if final:
        in_specs.append(pl.BlockSpec((1, D_MODEL), lambda i, f: (0, 0)))
        args.append(final_g)
    return pl.pallas_call(
        functools.partial(_ffn_kernel, final=final),
        grid=(nt + 1, nfc),
        in_specs=in_specs,
        out_specs=pl.BlockSpec((tm, D_MODEL), lambda i, f: (_mm_tile(i), 0)),
        out_shape=jax.ShapeDtypeStruct((rows, D_MODEL), F32),
        scratch_shapes=[pltpu.VMEM((2, tm, D_MODEL), BF16)],
        compiler_params=_cparams(("arbitrary", "arbitrary")),
        name="ffn_final" if final else "ffn",
    )(*args)


PROJ_CHUNKS = 4


def _proj_pipeline(x_ref, g_ref, sh_ref, sc_ref, h_ref, matmuls):
    i = pl.program_id(0)

    def chunk():
        _norm_chunk(x_ref, g_ref, sh_ref, sc_ref, h_ref, PROJ_CHUNKS)

    @pl.when(i == 0)
    def _():
        chunk()

    @pl.when(i > 0)
    def _():
        matmuls(h_ref[(i + 1) % 2], chunk)


def _q_scale(n_q):
    return jnp.where(pl.program_id(1) < n_q, ATTN_SCALE, 1.0).astype(F32)


def _proj_kernel(x_ref, g_ref, sh_ref, sc_ref, w_ref, o_ref, h_ref, *, n_q):
    def matmuls(h, chunk):
        y = jnp.dot(h, w_ref[...], preferred_element_type=F32)
        o_ref[...] = (y * _q_scale(n_q)).astype(o_ref.dtype)
        chunk()

    _proj_pipeline(x_ref, g_ref, sh_ref, sc_ref, h_ref, matmuls)


def _rope_swap(y):
    n = y.shape[-1]
    lane = lax.broadcasted_iota(jnp.int32, y.shape, 1)
    return jnp.where(lane % ROPE_AXIS_DIM < ROPE_AXIS_DIM // 2,
                     pltpu.roll(y, n - ROPE_AXIS_DIM // 2, 1),
                     pltpu.roll(y, ROPE_AXIS_DIM // 2, 1))


def _proj_rope_kernel(x_ref, g_ref, sh_ref, sc_ref, w_ref, cos_ref, sin_ref, o_ref, h_ref, *, n_q, n_rope):
    def matmuls(h, chunk):
        j = pl.program_id(1)
        y = jnp.dot(h, w_ref[...], preferred_element_type=F32)
        chunk()

        @pl.when(j < n_rope)
        def _():
            scale = _q_scale(n_q)
            cos = cos_ref[...] * scale
            sin = sin_ref[...] * scale
            for hh in range(y.shape[-1] // HEAD_DIM):
                yh = y[:, hh * HEAD_DIM:(hh + 1) * HEAD_DIM]
                o_ref[:, hh * HEAD_DIM:(hh + 1) * HEAD_DIM] = (yh * cos + _rope_swap(yh) * sin).astype(o_ref.dtype)

        @pl.when(j >= n_rope)
        def _():
            o_ref[...] = y.astype(o_ref.dtype)

    _proj_pipeline(x_ref, g_ref, sh_ref, sc_ref, h_ref, matmuls)


PROJ_TILES = R_ALL // TM_PROJ


def _proj_common_specs(layer):
    pro = _pro_tile(PROJ_TILES)
    return [
        pl.BlockSpec((TM_PROJ, D_MODEL), lambda i, j: (pro(i), 0)),
        pl.BlockSpec((1, D_MODEL), lambda i, j: (0, 0)),
        _mod_spec(layer, 3, TM_PROJ, pro),
        _mod_spec(layer, 4, TM_PROJ, pro),
    ]


def _proj_col_spec(rows, tn, first=0):
    return pl.BlockSpec((rows, tn), lambda i, j: (0, first + _mm_step(i, j)))


def _proj_out_spec(tn):
    return pl.BlockSpec((TM_PROJ, tn), lambda i, j: (_mm_tile(i), _mm_step(i, j)))


def _proj_call(kernel, name, tn, n_steps, in_specs, out_specs, out_shape, args):
    assert n_steps >= PROJ_CHUNKS
    return pl.pallas_call(
        kernel,
        grid=(PROJ_TILES + 1, n_steps),
        in_specs=in_specs,
        out_specs=out_specs,
        out_shape=out_shape,
        scratch_shapes=[pltpu.VMEM((2, TM_PROJ, D_MODEL), BF16)],
        compiler_params=_cparams(("arbitrary", "arbitrary")),
        name=name,
    )(*args)


def _proj(x, mods, layer, g, w, n_q_cols):
    tn = 1024
    n = w.shape[1]
    return _proj_call(
        functools.partial(_proj_kernel, n_q=n_q_cols // tn), "proj", tn, n // tn,
        _proj_common_specs(layer) + [_proj_col_spec(D_MODEL, tn)],
        _proj_out_spec(tn), jax.ShapeDtypeStruct((R_ALL, n), BF16),
        (x, g, mods, mods, w))


def _proj_rope(x, mods, layer, g, w, cos_t, sin_t, n_q_cols, n_rope_cols):
    tn = 512
    n = w.shape[1]
    n_lat = R_LAT // TM_PROJ
    tpb = SEQ // TM_PROJ

    def tab(i, j):
        t = _mm_tile(i)
        return (jnp.where(t < n_lat, t % tpb, tpb + t - n_lat), 0)

    return _proj_call(
        functools.partial(_proj_rope_kernel, n_q=n_q_cols // tn, n_rope=n_rope_cols // tn),
        "proj_rope", tn, n // tn,
        _proj_common_specs(layer) + [_proj_col_spec(D_MODEL, tn),
                                     pl.BlockSpec((TM_PROJ, HEAD_DIM), tab),
                                     pl.BlockSpec((TM_PROJ, HEAD_DIM), tab)],
        _proj_out_spec(tn), jax.ShapeDtypeStruct((R_ALL, n), BF16),
        (x, g, mods, mods, w, cos_t, sin_t))


def _proj_glu_kernel(x_ref, g_ref, sh_ref, sc_ref, wa_ref, wg_ref, ba_ref, bg_ref, o_ref, h_ref):
    def matmuls(h, chunk):
        a = jnp.dot(h, wa_ref[...], preferred_element_type=F32) + ba_ref[...]
        gt = jnp.dot(h, wg_ref[...], preferred_element_type=F32) + bg_ref[...]
        o_ref[...] = a * jax.nn.sigmoid(gt)
        chunk()

    _proj_pipeline(x_ref, g_ref, sh_ref, sc_ref, h_ref, matmuls)


def _proj_glu(x, mods, layer, g, w, b):
    tn = 512
    nc = D_MODEL // tn
    return _proj_call(
        _proj_glu_kernel, "proj_glu", tn, nc,
        _proj_common_specs(layer) + [_proj_col_spec(D_MODEL, tn), _proj_col_spec(D_MODEL, tn, nc),
                                     _proj_col_spec(1, tn), _proj_col_spec(1, tn, nc)],
        _proj_out_spec(tn), jax.ShapeDtypeStruct((R_ALL, D_MODEL), F32),
        (x, g, mods, mods, w, w, b, b))


def _proj_sc_kernel(x_ref, g_ref, sh_ref, sc_ref, wb_ref, wc_ref, wx_ref, bg_ref, p_ref, h_ref):
    def matmuls(h, chunk):
        bg_ref[...] = jnp.dot(h, wb_ref[...], preferred_element_type=F32).astype(bg_ref.dtype)
        cg = jnp.dot(h, wc_ref[...], preferred_element_type=F32)
        xin = jnp.dot(h, wx_ref[...], preferred_element_type=F32)
        p_ref[...] = cg * xin
        chunk()

    _proj_pipeline(x_ref, g_ref, sh_ref, sc_ref, h_ref, matmuls)


def _proj_sc(x, mods, layer, g, w):
    tn = 512
    nc = D_MODEL // tn
    return _proj_call(
        _proj_sc_kernel, "proj_sc", tn, nc,
        _proj_common_specs(layer) + [_proj_col_spec(D_MODEL, tn), _proj_col_spec(D_MODEL, tn, nc),
                                     _proj_col_spec(D_MODEL, tn, 2 * nc)],
        [_proj_out_spec(tn), _proj_out_spec(tn)],
        [jax.ShapeDtypeStruct((R_ALL, D_MODEL), BF16), jax.ShapeDtypeStruct((R_ALL, D_MODEL), F32)],
        (x, g, mods, mods, w, w, w))


def _outproj_kernel(*refs, n_lat, has_ctx, has_bias):
    refs = list(refs)
    ol_ref = refs.pop(0)
    oc_ref = refs.pop(0) if has_ctx else None
    w_ref = refs.pop(0)
    b_ref = refs.pop(0) if has_bias else None
    x_ref, gt_ref, o_ref = refs
    i = pl.program_id(0)

    def finish(src_ref):
        y = jnp.dot(src_ref[...], w_ref[...], preferred_element_type=F32)
        if has_bias:
            y = y + b_ref[...]
        o_ref[...] = x_ref[...] + gt_ref[...] * y

    if has_ctx:
        @pl.when(i < n_lat)
        def _():
            finish(ol_ref)

        @pl.when(i >= n_lat)
        def _():
            finish(oc_ref)
    else:
        finish(ol_ref)


def _outproj(o_lat, o_ctx, x, mods, layer, w, bias, rows):
    tm, tn = TM, D_MODEL
    n_lat = R_LAT // tm
    has_ctx = o_ctx is not None
    has_bias = bias is not None
    in_specs = []
    args = []
    if has_ctx:
        in_specs.append(pl.BlockSpec((tm, D_MODEL), lambda i, j: (jnp.minimum(i, n_lat - 1), 0)))
        in_specs.append(pl.BlockSpec((tm, D_MODEL), lambda i, j: (jnp.maximum(i - n_lat, 0), 0)))
        args += [o_lat, o_ctx]
    else:
        in_specs.append(pl.BlockSpec((tm, D_MODEL), lambda i, j: (i, 0)))
        args.append(o_lat)
    in_specs.append(pl.BlockSpec((D_MODEL, tn), lambda i, j: (0, j)))
    args.append(w)
    if has_bias:
        in_specs.append(pl.BlockSpec((1, tn), lambda i, j: (0, j)))
        args.append(bias)
    in_specs.append(pl.BlockSpec((tm, tn), lambda i, j: (i, j)))
    in_specs.append(_mod_spec(layer, 5, tm))
    args += [x, mods]
    return pl.pallas_call(
        functools.partial(_outproj_kernel, n_lat=n_lat, has_ctx=has_ctx, has_bias=has_bias),
        grid=(rows // tm, D_MODEL // tn),
        in_specs=in_specs,
        out_specs=pl.BlockSpec((tm, tn), lambda i, j: (i, j)),
        out_shape=jax.ShapeDtypeStruct((rows, D_MODEL), F32),
        compiler_params=_cparams(("parallel", "parallel")),
        name="outproj",
    )(*args)


GRID_ROWS = SEQ // GRID_W
NA_RB = 4
NA_KROWS = 12
NA_Q = NA_RB * GRID_W
NA_KEYS = NA_KROWS * GRID_W
NA_OFFS = (0, NA_RB, 2 * NA_RB)
NA_RPB_R = 2 * NA_WIN_R - 1
NA_RPB_C = 2 * NA_WIN_C - 1


def _na_key_row0(r0):
    return jnp.clip(r0 - NA_WIN_R // 2, 0, GRID_ROWS - NA_KROWS)


def _na_bias_table(rpb):
    lanes = 2 * GRID_W
    period = jnp.concatenate([rpb[..., NA_WIN_C - 1:],
                              jnp.zeros(rpb.shape[:2] + (lanes - NA_RPB_C,), rpb.dtype),
                              rpb[..., :NA_WIN_C - 1]], axis=-1)
    flat = jnp.tile(period, (1, 1, GRID_W))[..., :GRID_W * (lanes - 1)]
    toep = flat.reshape(N_HEADS, NA_RPB_R, GRID_W, lanes - 1)[..., :GRID_W]

    offs = jnp.asarray(NA_OFFS)[:, None, None]
    a = jnp.arange(NA_RB)[None, :, None]
    jr = jnp.arange(NA_KROWS)[None, None, :]
    rs_rel = jnp.stack([jnp.zeros((NA_RB, 1), jnp.int32),
                        jnp.arange(NA_RB)[:, None],
                        jnp.full((NA_RB, 1), NA_KROWS - NA_WIN_R, jnp.int32)])
    row_ok = (jr >= rs_rel) & (jr < rs_rel + NA_WIN_R)
    row_off = jnp.clip(jr - offs - a + NA_WIN_R - 1, 0, NA_RPB_R - 1)
    c = jnp.arange(GRID_W)[:, None]
    kc = jnp.arange(GRID_W)[None, :]
    cs = jnp.clip(c - NA_WIN_C // 2, 0, GRID_W - NA_WIN_C)
    col_ok = (kc >= cs) & (kc < cs + NA_WIN_C)
    tab = jnp.take(toep, row_off.reshape(-1), axis=1)
    tab = tab.reshape(N_HEADS, len(NA_OFFS), NA_RB, NA_KROWS, GRID_W, GRID_W)
    ok = row_ok[None, :, :, :, None, None] & col_ok[None, None, None, None, :, :]
    tab = jnp.where(ok, tab, NEG_INF)
    return tab.transpose(1, 0, 2, 4, 3, 5).reshape(len(NA_OFFS), N_HEADS, NA_Q, NA_KEYS).astype(F32)


_NT = (((1,), (1,)), ((), ()))


def _with_ones(v):
    return jnp.concatenate([v, jnp.ones_like(v)], axis=1)


def _na_kernel(q_ref, k_ref, v_ref, kc_ref, vc_ref, bias_ref, o_ref):
    kc = kc_ref[...]
    vc = vc_ref[...]

    def body(t, carry):
        r0 = t * NA_RB
        ks = _na_key_row0(r0)
        q0 = pl.multiple_of(r0 * GRID_W, NA_Q)
        k0 = pl.multiple_of(ks * GRID_W, GRID_W)
        q = q_ref[pl.ds(q0, NA_Q), :]
        kw = k_ref[pl.ds(k0, NA_KEYS), :]
        vw = v_ref[pl.ds(k0, NA_KEYS), :]
        bias = bias_ref[(r0 - ks) // NA_RB]
        s_loc = lax.dot_general(q, kw, _NT, preferred_element_type=F32) + bias
        s_ctx = lax.dot_general(q, kc, _NT, preferred_element_type=F32)
        m = jnp.maximum(jnp.max(s_loc, axis=-1, keepdims=True), jnp.max(s_ctx, axis=-1, keepdims=True))
        p_loc = jnp.exp(s_loc - m)
        p_ctx = jnp.exp(s_ctx - m)
        l = jnp.sum(p_loc, axis=-1, keepdims=True) + jnp.sum(p_ctx, axis=-1, keepdims=True)
        o = (jnp.dot(p_loc.astype(BF16), vw, preferred_element_type=F32)
             + jnp.dot(p_ctx.astype(BF16), vc, preferred_element_type=F32))
        o_ref[pl.ds(q0, NA_Q), :] = (o / l).astype(o_ref.dtype)
        return carry

    lax.fori_loop(0, GRID_ROWS // NA_RB, body, 0, unroll=2)


def _na_attention(qkv, bias_tab):
    h3 = N_HEADS
    ctx_blk = R_LAT // CTX_LEN
    return pl.pallas_call(
        _na_kernel,
        grid=(BATCH, N_HEADS),
        in_specs=[
            pl.BlockSpec((SEQ, HEAD_DIM), lambda b, h: (b, h)),
            pl.BlockSpec((SEQ, HEAD_DIM), lambda b, h: (b, h3 + h)),
            pl.BlockSpec((SEQ, HEAD_DIM), lambda b, h: (b, 2 * h3 + h)),
            pl.BlockSpec((CTX_LEN, HEAD_DIM), lambda b, h: (ctx_blk + b, h3 + h)),
            pl.BlockSpec((CTX_LEN, HEAD_DIM), lambda b, h: (ctx_blk + b, 2 * h3 + h)),
            pl.BlockSpec((len(NA_OFFS), None, NA_Q, NA_KEYS), lambda b, h: (0, h, 0, 0)),
        ],
        out_specs=pl.BlockSpec((SEQ, HEAD_DIM), lambda b, h: (b, h)),
        out_shape=jax.ShapeDtypeStruct((R_LAT, D_MODEL), BF16),
        compiler_params=_cparams(("parallel", "parallel")),
        name="na_attn",
    )(qkv, qkv, qkv, qkv, qkv, bias_tab)


def _ctx_attn_kernel(q_ref, k_ref, v_ref, o_ref):
    s = lax.dot_general(q_ref[...], k_ref[...], _NT, preferred_element_type=F32)
    p = jnp.exp(s - jnp.max(s, axis=-1, keepdims=True)).astype(BF16)
    ol = jnp.dot(p, _with_ones(v_ref[...]), preferred_element_type=F32)
    o_ref[...] = (ol[:, :HEAD_DIM] / ol[:, HEAD_DIM:]).astype(o_ref.dtype)


def _ctx_attention(qkv):
    h3 = N_HEADS
    ctx_blk = R_LAT // CTX_LEN
    return pl.pallas_call(
        _ctx_attn_kernel,
        grid=(BATCH, N_HEADS),
        in_specs=[
            pl.BlockSpec((CTX_LEN, HEAD_DIM), lambda b, h: (ctx_blk + b, h)),
            pl.BlockSpec((CTX_LEN, HEAD_DIM), lambda b, h: (ctx_blk + b, h3 + h)),
            pl.BlockSpec((CTX_LEN, HEAD_DIM), lambda b, h: (ctx_blk + b, 2 * h3 + h)),
        ],
        out_specs=pl.BlockSpec((CTX_LEN, HEAD_DIM), lambda b, h: (b, h)),
        out_shape=jax.ShapeDtypeStruct((R_CTX, D_MODEL), BF16),
        compiler_params=_cparams(("parallel", "parallel")),
        name="ctx_attn",
    )(qkv, qkv, qkv)


SWA_G = N_HEADS // N_KV_HEADS
SWA_KEYS = 3 * SWA_BLOCK


SWA_NQ = 4


def _swa_mask_table():
    shape = (3, SWA_G * SWA_BLOCK, SWA_KEYS)
    lead = lax.broadcasted_iota(jnp.int32, shape, 0) * SWA_BLOCK
    qoff = lax.broadcasted_iota(jnp.int32, shape, 1) % SWA_BLOCK
    koff = lax.broadcasted_iota(jnp.int32, shape, 2)
    return jnp.where(jnp.abs(koff - lead - qoff) <= SWA_WINDOW, 0.0, NEG_INF).astype(F32)


def _swa_kernel(q_ref, k_ref, v_ref, kc_ref, vc_ref, sink_ref, mask_ref, o_ref):
    kc = kc_ref[...]
    vc1 = _with_ones(vc_ref[...])
    sink = sink_ref[...]
    for qi in range(SWA_NQ):
        n = pl.program_id(2) * SWA_NQ + qi
        ws = pl.multiple_of(jnp.clip((n - 1) * SWA_BLOCK, 0, SEQ - SWA_KEYS), SWA_BLOCK)
        rows = slice(qi * SWA_BLOCK, (qi + 1) * SWA_BLOCK)
        q = jnp.concatenate([q_ref[rows, g * HEAD_DIM:(g + 1) * HEAD_DIM] for g in range(SWA_G)], axis=0)
        kw = k_ref[pl.ds(ws, SWA_KEYS), :]
        vw = v_ref[pl.ds(ws, SWA_KEYS), :]
        s_loc = lax.dot_general(q, kw, _NT, preferred_element_type=F32) + mask_ref[n - ws // SWA_BLOCK]
        s_ctx = lax.dot_general(q, kc, _NT, preferred_element_type=F32)
        m = jnp.maximum(jnp.maximum(jnp.max(s_loc, axis=-1, keepdims=True),
                                    jnp.max(s_ctx, axis=-1, keepdims=True)), sink)
        p_loc = jnp.exp(s_loc - m).astype(BF16)
        p_ctx = jnp.exp(s_ctx - m).astype(BF16)
        ol = (jnp.dot(p_loc, _with_ones(vw), preferred_element_type=F32)
              + jnp.dot(p_ctx, vc1, preferred_element_type=F32))
        o = ol[:, :HEAD_DIM] / (ol[:, HEAD_DIM:] + jnp.exp(sink - m))
        for g in range(SWA_G):
            o_ref[rows, g * HEAD_DIM:(g + 1) * HEAD_DIM] = o[g * SWA_BLOCK:(g + 1) * SWA_BLOCK].astype(o_ref.dtype)


def _swa_attention(qkv, sink_rows):
    nb = SEQ // (SWA_BLOCK * SWA_NQ)
    kcol = N_HEADS
    vcol = N_HEADS + N_KV_HEADS
    ctx_blk = R_LAT // CTX_LEN
    return pl.pallas_call(
        _swa_kernel,
        grid=(BATCH, N_KV_HEADS, nb),
        in_specs=[
            pl.BlockSpec((SWA_NQ * SWA_BLOCK, SWA_G * HEAD_DIM), lambda b, kv, n: (b * nb + n, kv)),
            pl.BlockSpec((SEQ, HEAD_DIM), lambda b, kv, n: (b, kcol + kv)),
            pl.BlockSpec((SEQ, HEAD_DIM), lambda b, kv, n: (b, vcol + kv)),
            pl.BlockSpec((CTX_LEN, HEAD_DIM), lambda b, kv, n: (ctx_blk + b, kcol + kv)),
            pl.BlockSpec((CTX_LEN, HEAD_DIM), lambda b, kv, n: (ctx_blk + b, vcol + kv)),
            pl.BlockSpec((None, SWA_G * SWA_BLOCK, 1), lambda b, kv, n: (kv, 0, 0)),
            pl.BlockSpec((3, SWA_G * SWA_BLOCK, SWA_KEYS), lambda b, kv, n: (0, 0, 0)),
        ],
        out_specs=pl.BlockSpec((SWA_NQ * SWA_BLOCK, SWA_G * HEAD_DIM), lambda b, kv, n: (b * nb + n, kv)),
        out_shape=jax.ShapeDtypeStruct((R_LAT, D_MODEL), BF16),
        compiler_params=_cparams(("parallel", "parallel", "arbitrary")),
        name="swa_attn",
    )(qkv, qkv, qkv, qkv, qkv, sink_rows, _swa_mask_table())


CONV_HALO = 16
CONV_LANES = 128
CONV_ROWS = 64
SUBLANES = 8


def _halo_flags(i):
    tiles_per_seq = SEQ // CONV_TILE
    latent = i < R_LAT // CONV_TILE
    has_prev = jnp.logical_and(latent, i % tiles_per_seq != 0)
    has_next = jnp.logical_and(latent, i % tiles_per_seq != tiles_per_seq - 1)
    return has_prev, has_next


def _stage_rows(prev_ref, cur_ref, next_ref, stage_ref, i):
    has_prev, has_next = _halo_flags(i)
    stage_ref[0:CONV_HALO, :] = jnp.where(has_prev, prev_ref[...], 0.0)
    stage_ref[CONV_HALO:CONV_HALO + CONV_TILE, :] = cur_ref[...]
    stage_ref[CONV_HALO + CONV_TILE:, :] = jnp.where(has_next, next_ref[...], 0.0)


def _dwconv(stage_ref, w_ref, width, emit):
    base = CONV_HALO - width // 2

    def lane_chunk(ci, carry):
        lanes = pl.ds(pl.multiple_of(ci * CONV_LANES, CONV_LANES), CONV_LANES)
        w = w_ref[:, lanes]
        for r0 in range(0, CONV_TILE, CONV_ROWS):
            acc = None
            for phase in range(SUBLANES):
                part = None
                for k in range(width):
                    if (base + k) % SUBLANES != phase:
                        continue
                    start = r0 + base + k - phase
                    term = stage_ref[start:start + CONV_ROWS + SUBLANES, lanes] * w[k:k + 1, :]
                    part = term if part is None else part + term
                if part is not None:
                    part = part[phase:phase + CONV_ROWS]
                    acc = part if acc is None else acc + part
            emit(r0, lanes, acc)
        return carry

    lax.fori_loop(0, D_MODEL // CONV_LANES, lane_chunk, 0)


def _cv_conv_kernel(prev_ref, cur_ref, next_ref, w_ref, b_ref, lg_ref, lb_ref, o_ref, stage_ref, y_ref):
    i = pl.program_id(0)
    _stage_rows(prev_ref, cur_ref, next_ref, stage_ref, i)

    def emit(r0, lanes, y):
        y_ref[r0:r0 + CONV_ROWS, lanes] = y + b_ref[:, lanes]

    _dwconv(stage_ref, w_ref, CONV_WIDTH, emit)
    y = y_ref[...]
    mu = jnp.mean(y, axis=-1, keepdims=True)
    yc = y - mu
    var = jnp.mean(yc * yc, axis=-1, keepdims=True)
    z = yc * lax.rsqrt(var + EPS) * lg_ref[...] + lb_ref[...]
    o_ref[...] = _silu(z).astype(o_ref.dtype)


def _halo_specs():
    per = CONV_TILE // CONV_HALO
    last = R_ALL // CONV_HALO - 1
    return [
        pl.BlockSpec((CONV_HALO, D_MODEL), lambda i: (jnp.maximum(i * per - 1, 0), 0)),
        pl.BlockSpec((CONV_TILE, D_MODEL), lambda i: (i, 0)),
        pl.BlockSpec((CONV_HALO, D_MODEL), lambda i: (jnp.minimum((i + 1) * per, last), 0)),
    ]


def _cv_conv(z, w_dw, b_dw, ln_g, ln_b):
    vec = pl.BlockSpec((1, D_MODEL), lambda i: (0, 0))
    return pl.pallas_call(
        _cv_conv_kernel,
        grid=(R_ALL // CONV_TILE,),
        in_specs=_halo_specs() + [pl.BlockSpec((CONV_WIDTH, D_MODEL), lambda i: (0, 0)), vec, vec, vec],
        out_specs=pl.BlockSpec((CONV_TILE, D_MODEL), lambda i: (i, 0)),
        out_shape=jax.ShapeDtypeStruct((R_ALL, D_MODEL), BF16),
        scratch_shapes=[pltpu.VMEM((CONV_TILE + 2 * CONV_HALO, D_MODEL), F32),
                        pltpu.VMEM((CONV_TILE, D_MODEL), F32)],
        compiler_params=_cparams(("parallel",)),
        name="cv_conv",
    )(z, z, z, w_dw, b_dw, ln_g, ln_b)


def _sc_conv_kernel(prev_ref, cur_ref, next_ref, bg_ref, w_ref, o_ref, stage_ref):
    i = pl.program_id(0)
    _stage_rows(prev_ref, cur_ref, next_ref, stage_ref, i)

    def emit(r0, lanes, y):
        o_ref[r0:r0 + CONV_ROWS, lanes] = (bg_ref[r0:r0 + CONV_ROWS, lanes].astype(F32) * y).astype(o_ref.dtype)

    _dwconv(stage_ref, w_ref, SHORT_CONV_WIDTH, emit)


def _sc_conv(bg, p, w_conv):
    return pl.pallas_call(
        _sc_conv_kernel,
        grid=(R_ALL // CONV_TILE,),
        in_specs=_halo_specs() + [pl.BlockSpec((CONV_TILE, D_MODEL), lambda i: (i, 0)),
                                  pl.BlockSpec((SHORT_CONV_WIDTH, D_MODEL), lambda i: (0, 0))],
        out_specs=pl.BlockSpec((CONV_TILE, D_MODEL), lambda i: (i, 0)),
        out_shape=jax.ShapeDtypeStruct((R_ALL, D_MODEL), BF16),
        scratch_shapes=[pltpu.VMEM((CONV_TILE + 2 * CONV_HALO, D_MODEL), F32)],
        compiler_params=_cparams(("parallel",)),
        name="sc_conv",
    )(p, p, p, bg, w_conv)


def _rope_tables():
    t = jnp.arange(SEQ)
    quarter = ROPE_AXIS_DIM // 2
    inv_freq = jnp.power(ROPE_BASE, -jnp.arange(quarter, dtype=F32) / quarter)
    ang_r = (t // GRID_W).astype(F32)[:, None] * inv_freq[None, :]
    ang_c = (t % GRID_W).astype(F32)[:, None] * inv_freq[None, :]
    cos = jnp.concatenate([jnp.cos(ang_r), jnp.cos(ang_r), jnp.cos(ang_c), jnp.cos(ang_c)], axis=-1)
    sin = jnp.concatenate([-jnp.sin(ang_r), jnp.sin(ang_r), -jnp.sin(ang_c), jnp.sin(ang_c)], axis=-1)
    cos = jnp.concatenate([cos, jnp.ones((R_CTX, HEAD_DIM), F32)], axis=0)
    sin = jnp.concatenate([sin, jnp.zeros((R_CTX, HEAD_DIM), F32)], axis=0)
    return cos, sin


def kernel(x, c, ctx, c_ctx, w_mod, b_mod, norm_g, ffn_w_in, ffn_w_out, na_w_qkv, na_w_o, na_rpb,
           cv_w_pw1, cv_b_pw1, cv_w_dw, cv_b_dw, cv_ln_g, cv_ln_b, cv_w_pw2, cv_b_pw2,
           sc_w_in, sc_w_conv, sc_w_out, swa_w_qkv, swa_w_o, swa_sink, final_g):
    D = D_MODEL
    xa = jnp.concatenate([x.reshape(R_LAT, D), ctx.reshape(R_CTX, D)], axis=0)
    cc = jnp.concatenate([c, c_ctx[None, :], jnp.zeros((MOD_ROWS - BATCH - 1, D), F32)], axis=0)
    mods = _mods(cc, w_mod, b_mod).reshape(DEPTH, MOD_ROWS, N_MOD, 1, D)
    bf = lambda w: w.astype(BF16)
    row = lambda v: v.reshape(1, -1)
    w_in = bf(ffn_w_in)
    w_out = bf(ffn_w_out)

    for i in range(DEPTH):
        kind = i % 4
        last = i == DEPTH - 1
        g = norm_g[i]
        xa = _ffn(xa, mods, i, 0, row(g[0]), w_in, w_out, R_ALL)

        if kind == 0:
            qkv = _proj(xa, mods, i, row(g[1]), bf(na_w_qkv[0]), D)
            o_lat = _na_attention(qkv, _na_bias_table(na_rpb[0]))
            o_ctx = _ctx_attention(qkv)
            xa = _outproj(o_lat, o_ctx, xa, mods, i, bf(na_w_o[0]), None, R_ALL)
        elif kind == 1:
            z = _proj_glu(xa, mods, i, row(g[1]), bf(cv_w_pw1[0]), row(cv_b_pw1[0]))
            zc = _cv_conv(z, cv_w_dw[0], row(cv_b_dw[0]), row(cv_ln_g[0]), row(cv_ln_b[0]))
            xa = _outproj(zc, None, xa, mods, i, bf(cv_w_pw2[0]), row(cv_b_pw2[0]), R_ALL)
        elif kind == 2:
            bg, p = _proj_sc(xa, mods, i, row(g[1]), bf(sc_w_in[0]))
            y = _sc_conv(bg, p, sc_w_conv[0])
            xa = _outproj(y, None, xa, mods, i, bf(sc_w_out[0]), None, R_ALL)
        else:
            cos_t, sin_t = _rope_tables()
            qkv = _proj_rope(xa, mods, i, row(g[1]), bf(swa_w_qkv[0]), cos_t, sin_t,
                             D, D + N_KV_HEADS * HEAD_DIM)
            sink_rows = jnp.repeat(swa_sink[0].reshape(N_KV_HEADS, SWA_G), SWA_BLOCK, axis=1)[..., None]
            o_lat = _swa_attention(qkv, sink_rows)
            xa = _outproj(o_lat, None, xa, mods, i, bf(swa_w_o[0]), None, R_LAT)

        rows = R_LAT if last else R_ALL
        xa = _ffn(xa, mods, i, 1, row(g[2]), w_in, w_out, rows, final_g=row(final_g) if last else None)

    return xa.reshape(BATCH, SEQ, D)
```

```python
import functools

import jax
import jax.numpy as jnp
from jax import lax
from jax.experimental import pallas as pl
from jax.experimental.pallas import tpu as pltpu

D_MODEL = 2048
BATCH = 4
SEQ = 4096
DEPTH = 4
GRID_W = 64
CTX_LEN = 256
N_HEADS = 16
HEAD_DIM = D_MODEL // N_HEADS
N_KV_HEADS = 4
NA_WIN_R = 8
NA_WIN_C = 16
CONV_WIDTH = 31
SHORT_CONV_WIDTH = 3
SWA_WINDOW = 128
SWA_BLOCK = 128
D_FF = ((8 * D_MODEL // 3 + 255) // 256) * 256
ROPE_BASE = 10000.0
ROPE_AXIS_DIM = HEAD_DIM // 2
N_MOD = 9
MACARON_WEIGHT = 0.5
EPS = 1e-6
NEG_INF = -1e30

R_LAT = BATCH * SEQ
R_CTX = BATCH * CTX_LEN
R_ALL = R_LAT + R_CTX
MOD_ROWS = 8
ATTN_SCALE = HEAD_DIM ** -0.5

F32 = jnp.float32
BF16 = jnp.bfloat16

VMEM_LIMIT_BYTES = 56 * 1024 * 1024

TM = 512
TM_FFN = 1024
TM_PROJ = 1024
TF = 512
CONV_TILE = 256


def _cparams(sem):
    return pltpu.CompilerParams(dimension_semantics=sem, vmem_limit_bytes=VMEM_LIMIT_BYTES)


def _mod_row(i, tm):
    return jnp.minimum(i // (SEQ // tm), BATCH)


def _mod_spec(layer, k, tm, tile=lambda i: i):
    return pl.BlockSpec((None, None, None, 1, D_MODEL),
                        lambda i, j: (layer, _mod_row(tile(i), tm), k, 0, 0))


def _pro_tile(n_tiles):
    return lambda i: jnp.minimum(i, n_tiles - 1)


def _mm_tile(i):
    return jnp.maximum(i - 1, 0)


def _mm_step(i, s):
    return jnp.where(i == 0, 0, s)


def _norm_chunk(x_ref, g_ref, sh_ref, sc_ref, h_ref, n_chunks):
    rows = x_ref.shape[0] // n_chunks
    r = pl.multiple_of(jnp.minimum(pl.program_id(1), n_chunks - 1) * rows, rows)
    h = _norm_mod(x_ref[pl.ds(r, rows), :], g_ref[...], sh_ref[...], sc_ref[...])
    h_ref[pl.program_id(0) % 2, pl.ds(r, rows), :] = h.astype(BF16)


def _rms(x, g):
    return x * lax.rsqrt(jnp.mean(x * x, axis=-1, keepdims=True) + EPS) * g


def _norm_mod(x, g, shift, scale):
    return _rms(x, g) * (1.0 + scale) + shift


def _silu(x):
    return x * jax.nn.sigmoid(x)


def _mods_kernel(cc_ref, w_ref, b_ref, o_ref):
    s = _silu(cc_ref[...]).astype(BF16)
    o_ref[...] = jnp.dot(s, w_ref[...].astype(BF16), preferred_element_type=F32) + b_ref[...]


def _mods(cc, w_mod, b_mod):
    tn = 1024
    n = N_MOD * D_MODEL
    return pl.pallas_call(
        _mods_kernel,
        grid=(DEPTH, n // tn),
        in_specs=[
            pl.BlockSpec((MOD_ROWS, D_MODEL), lambda l, j: (0, 0)),
            pl.BlockSpec((None, D_MODEL, tn), lambda l, j: (l, 0, j)),
            pl.BlockSpec((None, 1, tn), lambda l, j: (l, 0, j)),
        ],
        out_specs=pl.BlockSpec((None, MOD_ROWS, tn), lambda l, j: (l, 0, j)),
        out_shape=jax.ShapeDtypeStruct((DEPTH, MOD_ROWS, n), F32),
        compiler_params=_cparams(("parallel", "parallel")),
        name="mods",
    )(cc, w_mod, b_mod.reshape(DEPTH, 1, n))


FFN_CHUNKS = 8


def _ffn_kernel(*refs, final):
    if final:
        xc_ref, g_ref, sh_ref, sc_ref, gt_ref, wg_ref, wu_ref, wo_ref, fg_ref, o_ref, h_ref, xk_ref = refs
    else:
        xc_ref, g_ref, sh_ref, sc_ref, gt_ref, wg_ref, wu_ref, wo_ref, o_ref, h_ref, xk_ref = refs
    i = pl.program_id(0)
    f = pl.program_id(1)
    rows = xc_ref.shape[0]

    def prologue_chunk():
        r = pl.multiple_of(jnp.minimum(f, FFN_CHUNKS - 1) * rows, rows)
        xs = xc_ref[...]
        xk_ref[pl.ds(r, rows), :] = xs
        h_ref[i % 2, pl.ds(r, rows), :] = _norm_mod(xs, g_ref[...], sh_ref[...], sc_ref[...]).astype(BF16)

    @pl.when(i == 0)
    def _():
        prologue_chunk()

    @pl.when(i > 0)
    def _():
        @pl.when(f == 0)
        def _():
            o_ref[...] = xk_ref[...]

        h = h_ref[(i + 1) % 2]
        gate = jnp.dot(h, wg_ref[...], preferred_element_type=F32)
        up = jnp.dot(h, wu_ref[...], preferred_element_type=F32)
        a = (_silu(gate) * up).astype(BF16)
        o_ref[...] += (MACARON_WEIGHT * gt_ref[...]) * jnp.dot(a, wo_ref[...], preferred_element_type=F32)
        prologue_chunk()

        if final:
            @pl.when(f == pl.num_programs(1) - 1)
            def _():
                o_ref[...] = _rms(o_ref[...], fg_ref[...])


def _ffn(x, mods, layer, half, g, w_in, w_out, rows, final_g=None):
    tm, tf = TM_FFN, TF
    nfc = D_FF // tf
    assert nfc >= FFN_CHUNKS
    nt = rows // tm
    k0 = 6 * half
    final = final_g is not None
    pro = _pro_tile(nt)
    in_specs = [
        pl.BlockSpec((tm // FFN_CHUNKS, D_MODEL),
                     lambda i, f: (pro(i) * FFN_CHUNKS + jnp.minimum(f, FFN_CHUNKS - 1), 0)),
        pl.BlockSpec((1, D_MODEL), lambda i, f: (0, 0)),
        _mod_spec(layer, k0, tm, pro),
        _mod_spec(layer, k0 + 1, tm, pro),
        _mod_spec(layer, k0 + 2, tm, _mm_tile),
        pl.BlockSpec((None, None, D_MODEL, tf), lambda i, f: (layer, half, 0, _mm_step(i, f))),
        pl.BlockSpec((None, None, D_MODEL, tf), lambda i, f: (layer, half, 0, nfc + _mm_step(i, f))),
        pl.BlockSpec((None, None, tf, D_MODEL), lambda i, f: (layer, half, _mm_step(i, f), 0)),
    ]
    args = [x, g, mods, mods, mods, w_in, w_in, w_out]
    if final:
        in_specs.append(pl.BlockSpec((1, D_MODEL), lambda i, f: (0, 0)))
        args.append(final_g)
    return pl.pallas_call(
        functools.partial(_ffn_kernel, final=final),
        grid=(nt + 1, nfc),
        in_specs=in_specs,
        out_specs=pl.BlockSpec((tm, D_MODEL), lambda i, f: (_mm_tile(i), 0)),
        out_shape=jax.ShapeDtypeStruct((rows, D_MODEL), F32),
        scratch_shapes=[pltpu.VMEM((2, tm, D_MODEL), BF16), pltpu.VMEM((tm, D_MODEL), F32)],
        compiler_params=_cparams(("arbitrary", "arbitrary")),
        name="ffn_final" if final else "ffn",
    )(*args)


PROJ_CHUNKS = 4


def _proj_pipeline(x_ref, g_ref, sh_ref, sc_ref, h_ref, matmuls):
    i = pl.program_id(0)

    def chunk():
        _norm_chunk(x_ref, g_ref, sh_ref, sc_ref, h_ref, PROJ_CHUNKS)

    @pl.when(i == 0)
    def _():
        chunk()

    @pl.when(i > 0)
    def _():
        matmuls(h_ref[(i + 1) % 2], chunk)


def _q_scale(n_q):
    return jnp.where(pl.program_id(1) < n_q, ATTN_SCALE, 1.0).astype(F32)


def _proj_kernel(x_ref, g_ref, sh_ref, sc_ref, w_ref, o_ref, h_ref, *, n_q):
    def matmuls(h, chunk):
        y = jnp.dot(h, w_ref[...], preferred_element_type=F32)
        o_ref[...] = (y * _q_scale(n_q)).astype(o_ref.dtype)
        chunk()

    _proj_pipeline(x_ref, g_ref, sh_ref, sc_ref, h_ref, matmuls)


def _rope_swap(y):
    n = y.shape[-1]
    lane = lax.broadcasted_iota(jnp.int32, y.shape, 1)
    return jnp.where(lane % ROPE_AXIS_DIM < ROPE_AXIS_DIM // 2,
                     pltpu.roll(y, n - ROPE_AXIS_DIM // 2, 1),
                     pltpu.roll(y, ROPE_AXIS_DIM // 2, 1))


def _proj_rope_kernel(x_ref, g_ref, sh_ref, sc_ref, w_ref, cos_ref, sin_ref, o_ref, h_ref, *, n_q, n_rope):
    def matmuls(h, chunk):
        j = pl.program_id(1)
        y = jnp.dot(h, w_ref[...], preferred_element_type=F32)
        chunk()

        @pl.when(j < n_rope)
        def _():
            scale = _q_scale(n_q)
            cos = cos_ref[...] * scale
            sin = sin_ref[...] * scale
            for hh in range(y.shape[-1] // HEAD_DIM):
                yh = y[:, hh * HEAD_DIM:(hh + 1) * HEAD_DIM]
                o_ref[:, hh * HEAD_DIM:(hh + 1) * HEAD_DIM] = (yh * cos + _rope_swap(yh) * sin).astype(o_ref.dtype)

        @pl.when(j >= n_rope)
        def _():
            o_ref[...] = y.astype(o_ref.dtype)

    _proj_pipeline(x_ref, g_ref, sh_ref, sc_ref, h_ref, matmuls)


PROJ_TILES = R_ALL // TM_PROJ


def _proj_common_specs(layer):
    pro = _pro_tile(PROJ_TILES)
    return [
        pl.BlockSpec((TM_PROJ, D_MODEL), lambda i, j: (pro(i), 0)),
        pl.BlockSpec((1, D_MODEL), lambda i, j: (0, 0)),
        _mod_spec(layer, 3, TM_PROJ, pro),
        _mod_spec(layer, 4, TM_PROJ, pro),
    ]


def _proj_col_spec(rows, tn, first=0):
    return pl.BlockSpec((rows, tn), lambda i, j: (0, first + _mm_step(i, j)))


def _proj_out_spec(tn):
    return pl.BlockSpec((TM_PROJ, tn), lambda i, j: (_mm_tile(i), _mm_step(i, j)))


def _proj_call(kernel, name, tn, n_steps, in_specs, out_specs, out_shape, args):
    assert n_steps >= PROJ_CHUNKS
    return pl.pallas_call(
        kernel,
        grid=(PROJ_TILES + 1, n_steps),
        in_specs=in_specs,
        out_specs=out_specs,
        out_shape=out_shape,
        scratch_shapes=[pltpu.VMEM((2, TM_PROJ, D_MODEL), BF16)],
        compiler_params=_cparams(("arbitrary", "arbitrary")),
        name=name,
    )(*args)


def _proj(x, mods, layer, g, w, n_q_cols):
    tn = 1024
    n = w.shape[1]
    return _proj_call(
        functools.partial(_proj_kernel, n_q=n_q_cols // tn), "proj", tn, n // tn,
        _proj_common_specs(layer) + [_proj_col_spec(D_MODEL, tn)],
        _proj_out_spec(tn), jax.ShapeDtypeStruct((R_ALL, n), BF16),
        (x, g, mods, mods, w))


def _proj_rope(x, mods, layer, g, w, cos_t, sin_t, n_q_cols, n_rope_cols):
    tn = 512
    n = w.shape[1]
    n_lat = R_LAT // TM_PROJ
    tpb = SEQ // TM_PROJ

    def tab(i, j):
        t = _mm_tile(i)
        return (jnp.where(t < n_lat, t % tpb, tpb + t - n_lat), 0)

    return _proj_call(
        functools.partial(_proj_rope_kernel, n_q=n_q_cols // tn, n_rope=n_rope_cols // tn),
        "proj_rope", tn, n // tn,
        _proj_common_specs(layer) + [_proj_col_spec(D_MODEL, tn),
                                     pl.BlockSpec((TM_PROJ, HEAD_DIM), tab),
                                     pl.BlockSpec((TM_PROJ, HEAD_DIM), tab)],
        _proj_out_spec(tn), jax.ShapeDtypeStruct((R_ALL, n), BF16),
        (x, g, mods, mods, w, cos_t, sin_t))


def _proj_glu_kernel(x_ref, g_ref, sh_ref, sc_ref, wa_ref, wg_ref, ba_ref, bg_ref, o_ref, h_ref):
    def matmuls(h, chunk):
        a = jnp.dot(h, wa_ref[...], preferred_element_type=F32) + ba_ref[...]
        gt = jnp.dot(h, wg_ref[...], preferred_element_type=F32) + bg_ref[...]
        o_ref[...] = a * jax.nn.sigmoid(gt)
        chunk()

    _proj_pipeline(x_ref, g_ref, sh_ref, sc_ref, h_ref, matmuls)


def _proj_glu(x, mods, layer, g, w, b):
    tn = 512
    nc = D_MODEL // tn
    return _proj_call(
        _proj_glu_kernel, "proj_glu", tn, nc,
        _proj_common_specs(layer) + [_proj_col_spec(D_MODEL, tn), _proj_col_spec(D_MODEL, tn, nc),
                                     _proj_col_spec(1, tn), _proj_col_spec(1, tn, nc)],
        _proj_out_spec(tn), jax.ShapeDtypeStruct((R_ALL, D_MODEL), F32),
        (x, g, mods, mods, w, w, b, b))


def _proj_sc_kernel(x_ref, g_ref, sh_ref, sc_ref, wb_ref, wc_ref, wx_ref, bg_ref, p_ref, h_ref):
    def matmuls(h, chunk):
        bg_ref[...] = jnp.dot(h, wb_ref[...], preferred_element_type=F32).astype(bg_ref.dtype)
        cg = jnp.dot(h, wc_ref[...], preferred_element_type=F32)
        xin = jnp.dot(h, wx_ref[...], preferred_element_type=F32)
        p_ref[...] = cg * xin
        chunk()

    _proj_pipeline(x_ref, g_ref, sh_ref, sc_ref, h_ref, matmuls)


def _proj_sc(x, mods, layer, g, w):
    tn = 512
    nc = D_MODEL // tn
    return _proj_call(
        _proj_sc_kernel, "proj_sc", tn, nc,
        _proj_common_specs(layer) + [_proj_col_spec(D_MODEL, tn), _proj_col_spec(D_MODEL, tn, nc),
                                     _proj_col_spec(D_MODEL, tn, 2 * nc)],
        [_proj_out_spec(tn), _proj_out_spec(tn)],
        [jax.ShapeDtypeStruct((R_ALL, D_MODEL), BF16), jax.ShapeDtypeStruct((R_ALL, D_MODEL), F32)],
        (x, g, mods, mods, w, w, w))


def _outproj_kernel(*refs, n_lat, has_ctx, has_bias):
    refs = list(refs)
    ol_ref = refs.pop(0)
    oc_ref = refs.pop(0) if has_ctx else None
    w_ref = refs.pop(0)
    b_ref = refs.pop(0) if has_bias else None
    x_ref, gt_ref, o_ref = refs
    i = pl.program_id(0)

    def finish(src_ref):
        y = jnp.dot(src_ref[...], w_ref[...], preferred_element_type=F32)
        if has_bias:
            y = y + b_ref[...]
        o_ref[...] = x_ref[...] + gt_ref[...] * y

    if has_ctx:
        @pl.when(i < n_lat)
        def _():
            finish(ol_ref)

        @pl.when(i >= n_lat)
        def _():
            finish(oc_ref)
    else:
        finish(ol_ref)


def _outproj(o_lat, o_ctx, x, mods, layer, w, bias, rows):
    tm, tn = TM, D_MODEL
    n_lat = R_LAT // tm
    has_ctx = o_ctx is not None
    has_bias = bias is not None
    in_specs = []
    args = []
    if has_ctx:
        in_specs.append(pl.BlockSpec((tm, D_MODEL), lambda i, j: (jnp.minimum(i, n_lat - 1), 0)))
        in_specs.append(pl.BlockSpec((tm, D_MODEL), lambda i, j: (jnp.maximum(i - n_lat, 0), 0)))
        args += [o_lat, o_ctx]
    else:
        in_specs.append(pl.BlockSpec((tm, D_MODEL), lambda i, j: (i, 0)))
        args.append(o_lat)
    in_specs.append(pl.BlockSpec((D_MODEL, tn), lambda i, j: (0, j)))
    args.append(w)
    if has_bias:
        in_specs.append(pl.BlockSpec((1, tn), lambda i, j: (0, j)))
        args.append(bias)
    in_specs.append(pl.BlockSpec((tm, tn), lambda i, j: (i, j)))
    in_specs.append(_mod_spec(layer, 5, tm))
    args += [x, mods]
    return pl.pallas_call(
        functools.partial(_outproj_kernel, n_lat=n_lat, has_ctx=has_ctx, has_bias=has_bias),
        grid=(rows // tm, D_MODEL // tn),
        in_specs=in_specs,
        out_specs=pl.BlockSpec((tm, tn), lambda i, j: (i, j)),
        out_shape=jax.ShapeDtypeStruct((rows, D_MODEL), F32),
        compiler_params=_cparams(("parallel", "parallel")),
        name="outproj",
    )(*args)


GRID_ROWS = SEQ // GRID_W
NA_RB = 4
NA_KROWS = 12
NA_Q = NA_RB * GRID_W
NA_KEYS = NA_KROWS * GRID_W
NA_OFFS = (0, NA_RB, 2 * NA_RB)
NA_RPB_R = 2 * NA_WIN_R - 1
NA_RPB_C = 2 * NA_WIN_C - 1


def _na_key_row0(r0):
    return jnp.clip(r0 - NA_WIN_R // 2, 0, GRID_ROWS - NA_KROWS)


def _na_bias_table(rpb):
    lanes = 2 * GRID_W
    period = jnp.concatenate([rpb[..., NA_WIN_C - 1:],
                              jnp.zeros(rpb.shape[:2] + (lanes - NA_RPB_C,), rpb.dtype),
                              rpb[..., :NA_WIN_C - 1]], axis=-1)
    flat = jnp.tile(period, (1, 1, GRID_W))[..., :GRID_W * (lanes - 1)]
    toep = flat.reshape(N_HEADS, NA_RPB_R, GRID_W, lanes - 1)[..., :GRID_W]

    offs = jnp.asarray(NA_OFFS)[:, None, None]
    a = jnp.arange(NA_RB)[None, :, None]
    jr = jnp.arange(NA_KROWS)[None, None, :]
    rs_rel = jnp.stack([jnp.zeros((NA_RB, 1), jnp.int32),
                        jnp.arange(NA_RB)[:, None],
                        jnp.full((NA_RB, 1), NA_KROWS - NA_WIN_R, jnp.int32)])
    row_ok = (jr >= rs_rel) & (jr < rs_rel + NA_WIN_R)
    row_off = jnp.clip(jr - offs - a + NA_WIN_R - 1, 0, NA_RPB_R - 1)
    c = jnp.arange(GRID_W)[:, None]
    kc = jnp.arange(GRID_W)[None, :]
    cs = jnp.clip(c - NA_WIN_C // 2, 0, GRID_W - NA_WIN_C)
    col_ok = (kc >= cs) & (kc < cs + NA_WIN_C)
    tab = jnp.take(toep, row_off.reshape(-1), axis=1)
    tab = tab.reshape(N_HEADS, len(NA_OFFS), NA_RB, NA_KROWS, GRID_W, GRID_W)
    ok = row_ok[None, :, :, :, None, None] & col_ok[None, None, None, None, :, :]
    tab = jnp.where(ok, tab, NEG_INF)
    return tab.transpose(1, 0, 2, 4, 3, 5).reshape(len(NA_OFFS), N_HEADS, NA_Q, NA_KEYS).astype(F32)


_NT = (((1,), (1,)), ((), ()))


def _with_ones(v):
    return jnp.concatenate([v, jnp.ones_like(v)], axis=1)


def _na_kernel(q_ref, k_ref, v_ref, kc_ref, vc_ref, bias_ref, o_ref):
    kc = kc_ref[...]
    vc = vc_ref[...]

    def body(t, carry):
        r0 = t * NA_RB
        ks = _na_key_row0(r0)
        q0 = pl.multiple_of(r0 * GRID_W, NA_Q)
        k0 = pl.multiple_of(ks * GRID_W, GRID_W)
        q = q_ref[pl.ds(q0, NA_Q), :]
        kw = k_ref[pl.ds(k0, NA_KEYS), :]
        vw = v_ref[pl.ds(k0, NA_KEYS), :]
        bias = bias_ref[(r0 - ks) // NA_RB]
        s_loc = lax.dot_general(q, kw, _NT, preferred_element_type=F32) + bias
        s_ctx = lax.dot_general(q, kc, _NT, preferred_element_type=F32)
        m = jnp.maximum(jnp.max(s_loc, axis=-1, keepdims=True), jnp.max(s_ctx, axis=-1, keepdims=True))
        p_loc = jnp.exp(s_loc - m)
        p_ctx = jnp.exp(s_ctx - m)
        l = jnp.sum(p_loc, axis=-1, keepdims=True) + jnp.sum(p_ctx, axis=-1, keepdims=True)
        o = (jnp.dot(p_loc.astype(BF16), vw, preferred_element_type=F32)
             + jnp.dot(p_ctx.astype(BF16), vc, preferred_element_type=F32))
        o_ref[pl.ds(q0, NA_Q), :] = (o / l).astype(o_ref.dtype)
        return carry

    lax.fori_loop(0, GRID_ROWS // NA_RB, body, 0, unroll=2)


def _na_attention(qkv, bias_tab):
    h3 = N_HEADS
    ctx_blk = R_LAT // CTX_LEN
    return pl.pallas_call(
        _na_kernel,
        grid=(BATCH, N_HEADS),
        in_specs=[
            pl.BlockSpec((SEQ, HEAD_DIM), lambda b, h: (b, h)),
            pl.BlockSpec((SEQ, HEAD_DIM), lambda b, h: (b, h3 + h)),
            pl.BlockSpec((SEQ, HEAD_DIM), lambda b, h: (b, 2 * h3 + h)),
            pl.BlockSpec((CTX_LEN, HEAD_DIM), lambda b, h: (ctx_blk + b, h3 + h)),
            pl.BlockSpec((CTX_LEN, HEAD_DIM), lambda b, h: (ctx_blk + b, 2 * h3 + h)),
            pl.BlockSpec((len(NA_OFFS), None, NA_Q, NA_KEYS), lambda b, h: (0, h, 0, 0)),
        ],
        out_specs=pl.BlockSpec((SEQ, HEAD_DIM), lambda b, h: (b, h)),
        out_shape=jax.ShapeDtypeStruct((R_LAT, D_MODEL), BF16),
        compiler_params=_cparams(("parallel", "parallel")),
        name="na_attn",
    )(qkv, qkv, qkv, qkv, qkv, bias_tab)


def _ctx_attn_kernel(q_ref, k_ref, v_ref, o_ref):
    s = lax.dot_general(q_ref[...], k_ref[...], _NT, preferred_element_type=F32)
    p = jnp.exp(s - jnp.max(s, axis=-1, keepdims=True)).astype(BF16)
    ol = jnp.dot(p, _with_ones(v_ref[...]), preferred_element_type=F32)
    o_ref[...] = (ol[:, :HEAD_DIM] / ol[:, HEAD_DIM:]).astype(o_ref.dtype)


def _ctx_attention(qkv):
    h3 = N_HEADS
    ctx_blk = R_LAT // CTX_LEN
    return pl.pallas_call(
        _ctx_attn_kernel,
        grid=(BATCH, N_HEADS),
        in_specs=[
            pl.BlockSpec((CTX_LEN, HEAD_DIM), lambda b, h: (ctx_blk + b, h)),
            pl.BlockSpec((CTX_LEN, HEAD_DIM), lambda b, h: (ctx_blk + b, h3 + h)),
            pl.BlockSpec((CTX_LEN, HEAD_DIM), lambda b, h: (ctx_blk + b, 2 * h3 + h)),
        ],
        out_specs=pl.BlockSpec((CTX_LEN, HEAD_DIM), lambda b, h: (b, h)),
        out_shape=jax.ShapeDtypeStruct((R_CTX, D_MODEL), BF16),
        compiler_params=_cparams(("parallel", "parallel")),
        name="ctx_attn",
    )(qkv, qkv, qkv)


SWA_G = N_HEADS // N_KV_HEADS
SWA_KEYS = 3 * SWA_BLOCK


SWA_NQ = 4


def _swa_mask_table():
    shape = (3, SWA_G * SWA_BLOCK, SWA_KEYS)
    lead = lax.broadcasted_iota(jnp.int32, shape, 0) * SWA_BLOCK
    qoff = lax.broadcasted_iota(jnp.int32, shape, 1) % SWA_BLOCK
    koff = lax.broadcasted_iota(jnp.int32, shape, 2)
    return jnp.where(jnp.abs(koff - lead - qoff) <= SWA_WINDOW, 0.0, NEG_INF).astype(F32)


def _swa_kernel(q_ref, k_ref, v_ref, kc_ref, vc_ref, sink_ref, mask_ref, o_ref):
    kc = kc_ref[...]
    vc1 = _with_ones(vc_ref[...])
    sink = sink_ref[...]
    for qi in range(SWA_NQ):
        n = pl.program_id(2) * SWA_NQ + qi
        ws = pl.multiple_of(jnp.clip((n - 1) * SWA_BLOCK, 0, SEQ - SWA_KEYS), SWA_BLOCK)
        rows = slice(qi * SWA_BLOCK, (qi + 1) * SWA_BLOCK)
        q = jnp.concatenate([q_ref[rows, g * HEAD_DIM:(g + 1) * HEAD_DIM] for g in range(SWA_G)], axis=0)
        kw = k_ref[pl.ds(ws, SWA_KEYS), :]
        vw = v_ref[pl.ds(ws, SWA_KEYS), :]
        s_loc = lax.dot_general(q, kw, _NT, preferred_element_type=F32) + mask_ref[n - ws // SWA_BLOCK]
        s_ctx = lax.dot_general(q, kc, _NT, preferred_element_type=F32)
        m = jnp.maximum(jnp.maximum(jnp.max(s_loc, axis=-1, keepdims=True),
                                    jnp.max(s_ctx, axis=-1, keepdims=True)), sink)
        p_loc = jnp.exp(s_loc - m).astype(BF16)
        p_ctx = jnp.exp(s_ctx - m).astype(BF16)
        ol = (jnp.dot(p_loc, _with_ones(vw), preferred_element_type=F32)
              + jnp.dot(p_ctx, vc1, preferred_element_type=F32))
        o = ol[:, :HEAD_DIM] / (ol[:, HEAD_DIM:] + jnp.exp(sink - m))
        for g in range(SWA_G):
            o_ref[rows, g * HEAD_DIM:(g + 1) * HEAD_DIM] = o[g * SWA_BLOCK:(g + 1) * SWA_BLOCK].astype(o_ref.dtype)


def _swa_attention(qkv, sink_rows):
    nb = SEQ // (SWA_BLOCK * SWA_NQ)
    kcol = N_HEADS
    vcol = N_HEADS + N_KV_HEADS
    ctx_blk = R_LAT // CTX_LEN
    return pl.pallas_call(
        _swa_kernel,
        grid=(BATCH, N_KV_HEADS, nb),
        in_specs=[
            pl.BlockSpec((SWA_NQ * SWA_BLOCK, SWA_G * HEAD_DIM), lambda b, kv, n: (b * nb + n, kv)),
            pl.BlockSpec((SEQ, HEAD_DIM), lambda b, kv, n: (b, kcol + kv)),
            pl.BlockSpec((SEQ, HEAD_DIM), lambda b, kv, n: (b, vcol + kv)),
            pl.BlockSpec((CTX_LEN, HEAD_DIM), lambda b, kv, n: (ctx_blk + b, kcol + kv)),
            pl.BlockSpec((CTX_LEN, HEAD_DIM), lambda b, kv, n: (ctx_blk + b, vcol + kv)),
            pl.BlockSpec((None, SWA_G * SWA_BLOCK, 1), lambda b, kv, n: (kv, 0, 0)),
            pl.BlockSpec((3, SWA_G * SWA_BLOCK, SWA_KEYS), lambda b, kv, n: (0, 0, 0)),
        ],
        out_specs=pl.BlockSpec((SWA_NQ * SWA_BLOCK, SWA_G * HEAD_DIM), lambda b, kv, n: (b * nb + n, kv)),
        out_shape=jax.ShapeDtypeStruct((R_LAT, D_MODEL), BF16),
        compiler_params=_cparams(("parallel", "parallel", "arbitrary")),
        name="swa_attn",
    )(qkv, qkv, qkv, qkv, qkv, sink_rows, _swa_mask_table())


CONV_HALO = 16
CONV_LANES = 128
CONV_ROWS = 64
SUBLANES = 8


def _halo_flags(i):
    tiles_per_seq = SEQ // CONV_TILE
    latent = i < R_LAT // CONV_TILE
    has_prev = jnp.logical_and(latent, i % tiles_per_seq != 0)
    has_next = jnp.logical_and(latent, i % tiles_per_seq != tiles_per_seq - 1)
    return has_prev, has_next


def _stage_rows(prev_ref, cur_ref, next_ref, stage_ref, i):
    has_prev, has_next = _halo_flags(i)
    stage_ref[0:CONV_HALO, :] = jnp.where(has_prev, prev_ref[...], 0.0)
    stage_ref[CONV_HALO:CONV_HALO + CONV_TILE, :] = cur_ref[...]
    stage_ref[CONV_HALO + CONV_TILE:, :] = jnp.where(has_next, next_ref[...], 0.0)


def _dwconv(stage_ref, w_ref, width, emit):
    base = CONV_HALO - width // 2

    def lane_chunk(ci, carry):
        lanes = pl.ds(pl.multiple_of(ci * CONV_LANES, CONV_LANES), CONV_LANES)
        w = w_ref[:, lanes]
        for r0 in range(0, CONV_TILE, CONV_ROWS):
            acc = None
            for phase in range(SUBLANES):
                part = None
                for k in range(width):
                    if (base + k) % SUBLANES != phase:
                        continue
                    start = r0 + base + k - phase
                    term = stage_ref[start:start + CONV_ROWS + SUBLANES, lanes] * w[k:k + 1, :]
                    part = term if part is None else part + term
                if part is not None:
                    part = part[phase:phase + CONV_ROWS]
                    acc = part if acc is None else acc + part
            emit(r0, lanes, acc)
        return carry

    lax.fori_loop(0, D_MODEL // CONV_LANES, lane_chunk, 0)


def _cv_conv_kernel(prev_ref, cur_ref, next_ref, w_ref, b_ref, lg_ref, lb_ref, o_ref, stage_ref, y_ref):
    i = pl.program_id(0)
    _stage_rows(prev_ref, cur_ref, next_ref, stage_ref, i)

    def emit(r0, lanes, y):
        y_ref[r0:r0 + CONV_ROWS, lanes] = y + b_ref[:, lanes]

    _dwconv(stage_ref, w_ref, CONV_WIDTH, emit)
    y = y_ref[...]
    mu = jnp.mean(y, axis=-1, keepdims=True)
    yc = y - mu
    var = jnp.mean(yc * yc, axis=-1, keepdims=True)
    z = yc * lax.rsqrt(var + EPS) * lg_ref[...] + lb_ref[...]
    o_ref[...] = _silu(z).astype(o_ref.dtype)


def _halo_specs():
    per = CONV_TILE // CONV_HALO
    last = R_ALL // CONV_HALO - 1
    return [
        pl.BlockSpec((CONV_HALO, D_MODEL), lambda i: (jnp.maximum(i * per - 1, 0), 0)),
        pl.BlockSpec((CONV_TILE, D_MODEL), lambda i: (i, 0)),
        pl.BlockSpec((CONV_HALO, D_MODEL), lambda i: (jnp.minimum((i + 1) * per, last), 0)),
    ]


def _cv_conv(z, w_dw, b_dw, ln_g, ln_b):
    vec = pl.BlockSpec((1, D_MODEL), lambda i: (0, 0))
    return pl.pallas_call(
        _cv_conv_kernel,
        grid=(R_ALL // CONV_TILE,),
        in_specs=_halo_specs() + [pl.BlockSpec((CONV_WIDTH, D_MODEL), lambda i: (0, 0)), vec, vec, vec],
        out_specs=pl.BlockSpec((CONV_TILE, D_MODEL), lambda i: (i, 0)),
        out_shape=jax.ShapeDtypeStruct((R_ALL, D_MODEL), BF16),
        scratch_shapes=[pltpu.VMEM((CONV_TILE + 2 * CONV_HALO, D_MODEL), F32),
                        pltpu.VMEM((CONV_TILE, D_MODEL), F32)],
        compiler_params=_cparams(("parallel",)),
        name="cv_conv",
    )(z, z, z, w_dw, b_dw, ln_g, ln_b)


def _sc_conv_kernel(prev_ref, cur_ref, next_ref, bg_ref, w_ref, o_ref, stage_ref):
    i = pl.program_id(0)
    _stage_rows(prev_ref, cur_ref, next_ref, stage_ref, i)

    def emit(r0, lanes, y):
        o_ref[r0:r0 + CONV_ROWS, lanes] = (bg_ref[r0:r0 + CONV_ROWS, lanes].astype(F32) * y).astype(o_ref.dtype)

    _dwconv(stage_ref, w_ref, SHORT_CONV_WIDTH, emit)


def _sc_conv(bg, p, w_conv):
    return pl.pallas_call(
        _sc_conv_kernel,
        grid=(R_ALL // CONV_TILE,),
        in_specs=_halo_specs() + [pl.BlockSpec((CONV_TILE, D_MODEL), lambda i: (i, 0)),
                                  pl.BlockSpec((SHORT_CONV_WIDTH, D_MODEL), lambda i: (0, 0))],
        out_specs=pl.BlockSpec((CONV_TILE, D_MODEL), lambda i: (i, 0)),
        out_shape=jax.ShapeDtypeStruct((R_ALL, D_MODEL), BF16),
        scratch_shapes=[pltpu.VMEM((CONV_TILE + 2 * CONV_HALO, D_MODEL), F32)],
        compiler_params=_cparams(("parallel",)),
        name="sc_conv",
    )(p, p, p, bg, w_conv)


def _rope_tables():
    t = jnp.arange(SEQ)
    quarter = ROPE_AXIS_DIM // 2
    inv_freq = jnp.power(ROPE_BASE, -jnp.arange(quarter, dtype=F32) / quarter)
    ang_r = (t // GRID_W).astype(F32)[:, None] * inv_freq[None, :]
    ang_c = (t % GRID_W).astype(F32)[:, None] * inv_freq[None, :]
    cos = jnp.concatenate([jnp.cos(ang_r), jnp.cos(ang_r), jnp.cos(ang_c), jnp.cos(ang_c)], axis=-1)
    sin = jnp.concatenate([-jnp.sin(ang_r), jnp.sin(ang_r), -jnp.sin(ang_c), jnp.sin(ang_c)], axis=-1)
    cos = jnp.concatenate([cos, jnp.ones((R_CTX, HEAD_DIM), F32)], axis=0)
    sin = jnp.concatenate([sin, jnp.zeros((R_CTX, HEAD_DIM), F32)], axis=0)
    return cos, sin


def kernel(x, c, ctx, c_ctx, w_mod, b_mod, norm_g, ffn_w_in, ffn_w_out, na_w_qkv, na_w_o, na_rpb,
           cv_w_pw1, cv_b_pw1, cv_w_dw, cv_b_dw, cv_ln_g, cv_ln_b, cv_w_pw2, cv_b_pw2,
           sc_w_in, sc_w_conv, sc_w_out, swa_w_qkv, swa_w_o, swa_sink, final_g):
    D = D_MODEL
    xa = jnp.concatenate([x.reshape(R_LAT, D), ctx.reshape(R_CTX, D)], axis=0)
    cc = jnp.concatenate([c, c_ctx[None, :], jnp.zeros((MOD_ROWS - BATCH - 1, D), F32)], axis=0)
    mods = _mods(cc, w_mod, b_mod).reshape(DEPTH, MOD_ROWS, N_MOD, 1, D)
    bf = lambda w: w.astype(BF16)
    row = lambda v: v.reshape(1, -1)
    w_in = bf(ffn_w_in)
    w_out = bf(ffn_w_out)

    for i in range(DEPTH):
        kind = i % 4
        last = i == DEPTH - 1
        g = norm_g[i]
        xa = _ffn(xa, mods, i, 0, row(g[0]), w_in, w_out, R_ALL)

        if kind == 0:
            qkv = _proj(xa, mods, i, row(g[1]), bf(na_w_qkv[0]), D)
            o_lat = _na_attention(qkv, _na_bias_table(na_rpb[0]))
            o_ctx = _ctx_attention(qkv)
            xa = _outproj(o_lat, o_ctx, xa, mods, i, bf(na_w_o[0]), None, R_ALL)
        elif kind == 1:
            z = _proj_glu(xa, mods, i, row(g[1]), bf(cv_w_pw1[0]), row(cv_b_pw1[0]))
            zc = _cv_conv(z, cv_w_dw[0], row(cv_b_dw[0]), row(cv_ln_g[0]), row(cv_ln_b[0]))
            xa = _outproj(zc, None, xa, mods, i, bf(cv_w_pw2[0]), row(cv_b_pw2[0]), R_ALL)
        elif kind == 2:
            bg, p = _proj_sc(xa, mods, i, row(g[1]), bf(sc_w_in[0]))
            y = _sc_conv(bg, p, sc_w_conv[0])
            xa = _outproj(y, None, xa, mods, i, bf(sc_w_out[0]), None, R_ALL)
        else:
            cos_t, sin_t = _rope_tables()
            qkv = _proj_rope(xa, mods, i, row(g[1]), bf(swa_w_qkv[0]), cos_t, sin_t,
                             D, D + N_KV_HEADS * HEAD_DIM)
            sink_rows = jnp.repeat(swa_sink[0].reshape(N_KV_HEADS, SWA_G), SWA_BLOCK, axis=1)[..., None]
            o_lat = _swa_attention(qkv, sink_rows)
            xa = _outproj(o_lat, None, xa, mods, i, bf(swa_w_o[0]), None, R_LAT)

        rows = R_LAT if last else R_ALL
        xa = _ffn(xa, mods, i, 1, row(g[2]), w_in, w_out, rows, final_g=row(final_g) if last else None)

    return xa.reshape(BATCH, SEQ, D)
```

```python
import functools

import jax
import jax.numpy as jnp
from jax import lax
from jax.experimental import pallas as pl
from jax.experimental.pallas import tpu as pltpu

D_MODEL = 2048
BATCH = 4
SEQ = 4096
DEPTH = 4
GRID_W = 64
CTX_LEN = 256
N_HEADS = 16
HEAD_DIM = D_MODEL // N_HEADS
N_KV_HEADS = 4
NA_WIN_R = 8
NA_WIN_C = 16
CONV_WIDTH = 31
SHORT_CONV_WIDTH = 3
SWA_WINDOW = 128
SWA_BLOCK = 128
D_FF = ((8 * D_MODEL // 3 + 255) // 256) * 256
ROPE_BASE = 10000.0
ROPE_AXIS_DIM = HEAD_DIM // 2
N_MOD = 9
MACARON_WEIGHT = 0.5
EPS = 1e-6
NEG_INF = -1e30

R_LAT = BATCH * SEQ
R_CTX = BATCH * CTX_LEN
R_ALL = R_LAT + R_CTX
MOD_ROWS = 8
ATTN_SCALE = HEAD_DIM ** -0.5

F32 = jnp.float32
BF16 = jnp.bfloat16

VMEM_LIMIT_BYTES = 56 * 1024 * 1024

TM = 512
TM_FFN = 1024
TM_PROJ = 1024
TF = 512
CONV_TILE = 256


def _cparams(sem):
    return pltpu.CompilerParams(dimension_semantics=sem, vmem_limit_bytes=VMEM_LIMIT_BYTES)


def _mod_row(i, tm):
    return jnp.minimum(i // (SEQ // tm), BATCH)


def _mod_spec(layer, k, tm, tile=lambda i: i):
    return pl.BlockSpec((None, None, None, 1, D_MODEL),
                        lambda i, j: (layer, _mod_row(tile(i), tm), k, 0, 0))


def _pro_tile(n_tiles):
    return lambda i: jnp.minimum(i, n_tiles - 1)


def _mm_tile(i):
    return jnp.maximum(i - 1, 0)


def _mm_step(i, s):
    return jnp.where(i == 0, 0, s)


def _norm_chunk(x_ref, g_ref, sh_ref, sc_ref, h_ref, slot, n_chunks):
    rows = x_ref.shape[0] // n_chunks
    r = pl.multiple_of(jnp.minimum(pl.program_id(1), n_chunks - 1) * rows, rows)
    h = _norm_mod(x_ref[pl.ds(r, rows), :], g_ref[...], sh_ref[...], sc_ref[...])
    h_ref[slot, pl.ds(r, rows), :] = h.astype(BF16)


def _pipelined_rows(first_row, other_rows):
    i = pl.program_id(0)

    @pl.when(i == 0)
    def _():
        first_row()

    @pl.when(i > 0)
    def _():
        other_rows((i + 1) % 2, i % 2)


def _rms(x, g):
    return x * lax.rsqrt(jnp.mean(x * x, axis=-1, keepdims=True) + EPS) * g


def _norm_mod(x, g, shift, scale):
    return _rms(x, g) * (1.0 + scale) + shift


def _silu(x):
    return x * jax.nn.sigmoid(x)


def _zero_row_after(y, n):
    bits = jnp.max(pltpu.bitcast(y, jnp.int32), axis=0, keepdims=True)
    folded = bits[:, :n]
    for c0 in range(n, y.shape[1], n):
        folded = jnp.maximum(folded, bits[:, c0:c0 + n])
    half = jnp.full(folded.shape, 16, jnp.int32)
    return lax.shift_right_logical(lax.shift_right_logical(folded, half), half).astype(F32)


def _mods_kernel(cc_ref, w_ref, b_ref, o_ref):
    s = _silu(cc_ref[...]).astype(BF16)
    o_ref[...] = jnp.dot(s, w_ref[...].astype(BF16), preferred_element_type=F32) + b_ref[...]


def _mods(cc, w_mod, b_mod):
    tn = 1024
    n = N_MOD * D_MODEL
    return pl.pallas_call(
        _mods_kernel,
        grid=(DEPTH, n // tn),
        in_specs=[
            pl.BlockSpec((MOD_ROWS, D_MODEL), lambda l, j: (0, 0)),
            pl.BlockSpec((None, D_MODEL, tn), lambda l, j: (l, 0, j)),
            pl.BlockSpec((None, 1, tn), lambda l, j: (l, 0, j)),
        ],
        out_specs=pl.BlockSpec((None, MOD_ROWS, tn), lambda l, j: (l, 0, j)),
        out_shape=jax.ShapeDtypeStruct((DEPTH, MOD_ROWS, n), F32),
        compiler_params=_cparams(("parallel", "parallel")),
        name="mods",
    )(cc, w_mod, b_mod.reshape(DEPTH, 1, n))


FFN_CHUNKS = 8


def _ffn_kernel(*refs, final):
    if final:
        xc_ref, g_ref, sh_ref, sc_ref, gt_ref, wg_ref, wu_ref, wo_ref, fg_ref, o_ref, h_ref, xk_ref = refs
    else:
        xc_ref, g_ref, sh_ref, sc_ref, gt_ref, wg_ref, wu_ref, wo_ref, o_ref, h_ref, xk_ref = refs
    i = pl.program_id(0)
    f = pl.program_id(1)
    rows = xc_ref.shape[0]

    def prologue_chunk(slot):
        r = pl.multiple_of(jnp.minimum(f, FFN_CHUNKS - 1) * rows, rows)
        xs = xc_ref[...]
        xk_ref[pl.ds(r, rows), :] = xs
        hn = _norm_mod(xs, g_ref[...], sh_ref[...], sc_ref[...])
        h_ref[slot, pl.ds(r, rows), :] = hn.astype(BF16)
        return hn

    def matmul_row(read_slot, write_slot):
        @pl.when(f == 0)
        def _():
            o_ref[...] = xk_ref[...]

        h = h_ref[read_slot]
        gate = jnp.dot(h, wg_ref[...], preferred_element_type=F32)
        up = jnp.dot(h, wu_ref[...], preferred_element_type=F32)
        zero = _zero_row_after(prologue_chunk(write_slot), gate.shape[1])
        gate = jnp.concatenate([gate[:SUBLANES] + zero, gate[SUBLANES:]], axis=0)
        a = (_silu(gate) * up).astype(BF16)
        o_ref[...] += (MACARON_WEIGHT * gt_ref[...]) * jnp.dot(a, wo_ref[...], preferred_element_type=F32)

        if final:
            @pl.when(f == pl.num_programs(1) - 1)
            def _():
                o_ref[...] = _rms(o_ref[...], fg_ref[...])

    _pipelined_rows(lambda: prologue_chunk(0), matmul_row)


def _ffn(x, mods, layer, half, g, w_in, w_out, rows, final_g=None):
    tm, tf = TM_FFN, TF
    nfc = D_FF // tf
    assert nfc >= FFN_CHUNKS
    nt = rows // tm
    k0 = 6 * half
    final = final_g is not None
    pro = _pro_tile(nt)
    in_specs = [
        pl.BlockSpec((tm // FFN_CHUNKS, D_MODEL),
                     lambda i, f: (pro(i) * FFN_CHUNKS + jnp.minimum(f, FFN_CHUNKS - 1), 0)),
        pl.BlockSpec((1, D_MODEL), lambda i, f: (0, 0)),
        _mod_spec(layer, k0, tm, pro),
        _mod_spec(layer, k0 + 1, tm, pro),
        _mod_spec(layer, k0 + 2, tm, _mm_tile),
        pl.BlockSpec((None, None, D_MODEL, tf), lambda i, f: (layer, half, 0, _mm_step(i, f))),
        pl.BlockSpec((None, None, D_MODEL, tf), lambda i, f: (layer, half, 0, nfc + _mm_step(i, f))),
        pl.BlockSpec((None, None, tf, D_MODEL), lambda i, f: (layer, half, _mm_step(i, f), 0)),
    ]
    args = [x, g, mods, mods, mods, w_in, w_in, w_out]
    if final:
        in_specs.append(pl.BlockSpec((1, D_MODEL), lambda i, f: (0, 0)))
        args.append(final_g)
    return pl.pallas_call(
        functools.partial(_ffn_kernel, final=final),
        grid=(nt + 1, nfc),
        in_specs=in_specs,
        out_specs=pl.BlockSpec((tm, D_MODEL), lambda i, f: (_mm_tile(i), 0)),
        out_shape=jax.ShapeDtypeStruct((rows, D_MODEL), F32),
        scratch_shapes=[pltpu.VMEM((2, tm, D_MODEL), BF16), pltpu.VMEM((tm, D_MODEL), F32)],
        compiler_params=_cparams(("arbitrary", "arbitrary")),
        name="ffn_final" if final else "ffn",
    )(*args)


PROJ_CHUNKS = 4


def _proj_pipeline(x_ref, g_ref, sh_ref, sc_ref, h_ref, matmuls):
    def chunk(slot):
        return _norm_chunk(x_ref, g_ref, sh_ref, sc_ref, h_ref, slot, PROJ_CHUNKS)

    _pipelined_rows(lambda: chunk(0),
                    lambda read_slot, write_slot: matmuls(h_ref[read_slot], lambda: chunk(write_slot)))


def _q_scale(n_q):
    return jnp.where(pl.program_id(1) < n_q, ATTN_SCALE, 1.0).astype(F32)


def _proj_kernel(x_ref, g_ref, sh_ref, sc_ref, w_ref, o_ref, h_ref, *, n_q):
    def matmuls(h, chunk):
        y = jnp.dot(h, w_ref[...], preferred_element_type=F32)
        o_ref[...] = (y * _q_scale(n_q)).astype(o_ref.dtype)
        chunk()

    _proj_pipeline(x_ref, g_ref, sh_ref, sc_ref, h_ref, matmuls)


def _rope_swap(y):
    n = y.shape[-1]
    lane = lax.broadcasted_iota(jnp.int32, y.shape, 1)
    return jnp.where(lane % ROPE_AXIS_DIM < ROPE_AXIS_DIM // 2,
                     pltpu.roll(y, n - ROPE_AXIS_DIM // 2, 1),
                     pltpu.roll(y, ROPE_AXIS_DIM // 2, 1))


def _proj_rope_kernel(x_ref, g_ref, sh_ref, sc_ref, w_ref, cos_ref, sin_ref, o_ref, h_ref, *, n_q, n_rope):
    def matmuls(h, chunk):
        j = pl.program_id(1)
        y = jnp.dot(h, w_ref[...], preferred_element_type=F32)
        chunk()

        @pl.when(j < n_rope)
        def _():
            scale = _q_scale(n_q)
            cos = cos_ref[...] * scale
            sin = sin_ref[...] * scale
            for hh in range(y.shape[-1] // HEAD_DIM):
                yh = y[:, hh * HEAD_DIM:(hh + 1) * HEAD_DIM]
                o_ref[:, hh * HEAD_DIM:(hh + 1) * HEAD_DIM] = (yh * cos + _rope_swap(yh) * sin).astype(o_ref.dtype)

        @pl.when(j >= n_rope)
        def _():
            o_ref[...] = y.astype(o_ref.dtype)

    _proj_pipeline(x_ref, g_ref, sh_ref, sc_ref, h_ref, matmuls)


PROJ_TILES = R_ALL // TM_PROJ


def _proj_common_specs(layer):
    pro = _pro_tile(PROJ_TILES)
    return [
        pl.BlockSpec((TM_PROJ, D_MODEL), lambda i, j: (pro(i), 0)),
        pl.BlockSpec((1, D_MODEL), lambda i, j: (0, 0)),
        _mod_spec(layer, 3, TM_PROJ, pro),
        _mod_spec(layer, 4, TM_PROJ, pro),
    ]


def _proj_col_spec(rows, tn, first=0):
    return pl.BlockSpec((rows, tn), lambda i, j: (0, first + _mm_step(i, j)))


def _proj_out_spec(tn):
    return pl.BlockSpec((TM_PROJ, tn), lambda i, j: (_mm_tile(i), _mm_step(i, j)))


def _proj_call(kernel, name, tn, n_steps, in_specs, out_specs, out_shape, args):
    assert n_steps >= PROJ_CHUNKS
    return pl.pallas_call(
        kernel,
        grid=(PROJ_TILES + 1, n_steps),
        in_specs=in_specs,
        out_specs=out_specs,
        out_shape=out_shape,
        scratch_shapes=[pltpu.VMEM((2, TM_PROJ, D_MODEL), BF16)],
        compiler_params=_cparams(("arbitrary", "arbitrary")),
        name=name,
    )(*args)


def _proj(x, mods, layer, g, w, n_q_cols):
    tn = 1024
    n = w.shape[1]
    return _proj_call(
        functools.partial(_proj_kernel, n_q=n_q_cols // tn), "proj", tn, n // tn,
        _proj_common_specs(layer) + [_proj_col_spec(D_MODEL, tn)],
        _proj_out_spec(tn), jax.ShapeDtypeStruct((R_ALL, n), BF16),
        (x, g, mods, mods, w))


def _proj_rope(x, mods, layer, g, w, cos_t, sin_t, n_q_cols, n_rope_cols):
    tn = 512
    n = w.shape[1]
    n_lat = R_LAT // TM_PROJ
    tpb = SEQ // TM_PROJ

    def tab(i, j):
        t = _mm_tile(i)
        return (jnp.where(t < n_lat, t % tpb, tpb + t - n_lat), 0)

    return _proj_call(
        functools.partial(_proj_rope_kernel, n_q=n_q_cols // tn, n_rope=n_rope_cols // tn),
        "proj_rope", tn, n // tn,
        _proj_common_specs(layer) + [_proj_col_spec(D_MODEL, tn),
                                     pl.BlockSpec((TM_PROJ, HEAD_DIM), tab),
                                     pl.BlockSpec((TM_PROJ, HEAD_DIM), tab)],
        _proj_out_spec(tn), jax.ShapeDtypeStruct((R_ALL, n), BF16),
        (x, g, mods, mods, w, cos_t, sin_t))


def _proj_glu_kernel(x_ref, g_ref, sh_ref, sc_ref, wa_ref, wg_ref, ba_ref, bg_ref, o_ref, h_ref):
    def matmuls(h, chunk):
        a = jnp.dot(h, wa_ref[...], preferred_element_type=F32) + ba_ref[...]
        gt = jnp.dot(h, wg_ref[...], preferred_element_type=F32) + bg_ref[...]
        o_ref[...] = a * jax.nn.sigmoid(gt)
        chunk()

    _proj_pipeline(x_ref, g_ref, sh_ref, sc_ref, h_ref, matmuls)


def _proj_glu(x, mods, layer, g, w, b):
    tn = 512
    nc = D_MODEL // tn
    return _proj_call(
        _proj_glu_kernel, "proj_glu", tn, nc,
        _proj_common_specs(layer) + [_proj_col_spec(D_MODEL, tn), _proj_col_spec(D_MODEL, tn, nc),
                                     _proj_col_spec(1, tn), _proj_col_spec(1, tn, nc)],
        _proj_out_spec(tn), jax.ShapeDtypeStruct((R_ALL, D_MODEL), F32),
        (x, g, mods, mods, w, w, b, b))


def _proj_sc_kernel(x_ref, g_ref, sh_ref, sc_ref, wb_ref, wc_ref, wx_ref, bg_ref, p_ref, h_ref):
    def matmuls(h, chunk):
        bg_ref[...] = jnp.dot(h, wb_ref[...], preferred_element_type=F32).astype(bg_ref.dtype)
        cg = jnp.dot(h, wc_ref[...], preferred_element_type=F32)
        xin = jnp.dot(h, wx_ref[...], preferred_element_type=F32)
        p_ref[...] = cg * xin
        chunk()

    _proj_pipeline(x_ref, g_ref, sh_ref, sc_ref, h_ref, matmuls)


def _proj_sc(x, mods, layer, g, w):
    tn = 512
    nc = D_MODEL // tn
    return _proj_call(
        _proj_sc_kernel, "proj_sc", tn, nc,
        _proj_common_specs(layer) + [_proj_col_spec(D_MODEL, tn), _proj_col_spec(D_MODEL, tn, nc),
                                     _proj_col_spec(D_MODEL, tn, 2 * nc)],
        [_proj_out_spec(tn), _proj_out_spec(tn)],
        [jax.ShapeDtypeStruct((R_ALL, D_MODEL), BF16), jax.ShapeDtypeStruct((R_ALL, D_MODEL), F32)],
        (x, g, mods, mods, w, w, w))


def _outproj_kernel(*refs, n_lat, has_ctx, has_bias):
    refs = list(refs)
    ol_ref = refs.pop(0)
    oc_ref = refs.pop(0) if has_ctx else None
    w_ref = refs.pop(0)
    b_ref = refs.pop(0) if has_bias else None
    x_ref, gt_ref, o_ref = refs
    i = pl.program_id(0)

    def finish(src_ref):
        y = jnp.dot(src_ref[...], w_ref[...], preferred_element_type=F32)
        if has_bias:
            y = y + b_ref[...]
        o_ref[...] = x_ref[...] + gt_ref[...] * y

    if has_ctx:
        @pl.when(i < n_lat)
        def _():
            finish(ol_ref)

        @pl.when(i >= n_lat)
        def _():
            finish(oc_ref)
    else:
        finish(ol_ref)


def _outproj(o_lat, o_ctx, x, mods, layer, w, bias, rows):
    tm, tn = TM, D_MODEL
    n_lat = R_LAT // tm
    has_ctx = o_ctx is not None
    has_bias = bias is not None
    in_specs = []
    args = []
    if has_ctx:
        in_specs.append(pl.BlockSpec((tm, D_MODEL), lambda i, j: (jnp.minimum(i, n_lat - 1), 0)))
        in_specs.append(pl.BlockSpec((tm, D_MODEL), lambda i, j: (jnp.maximum(i - n_lat, 0), 0)))
        args += [o_lat, o_ctx]
    else:
        in_specs.append(pl.BlockSpec((tm, D_MODEL), lambda i, j: (i, 0)))
        args.append(o_lat)
    in_specs.append(pl.BlockSpec((D_MODEL, tn), lambda i, j: (0, j)))
    args.append(w)
    if has_bias:
        in_specs.append(pl.BlockSpec((1, tn), lambda i, j: (0, j)))
        args.append(bias)
    in_specs.append(pl.BlockSpec((tm, tn), lambda i, j: (i, j)))
    in_specs.append(_mod_spec(layer, 5, tm))
    args += [x, mods]
    return pl.pallas_call(
        functools.partial(_outproj_kernel, n_lat=n_lat, has_ctx=has_ctx, has_bias=has_bias),
        grid=(rows // tm, D_MODEL // tn),
        in_specs=in_specs,
        out_specs=pl.BlockSpec((tm, tn), lambda i, j: (i, j)),
        out_shape=jax.ShapeDtypeStruct((rows, D_MODEL), F32),
        compiler_params=_cparams(("parallel", "parallel")),
        name="outproj",
    )(*args)


GRID_ROWS = SEQ // GRID_W
NA_RB = 4
NA_KROWS = 12
NA_Q = NA_RB * GRID_W
NA_KEYS = NA_KROWS * GRID_W
NA_OFFS = (0, NA_RB, 2 * NA_RB)
NA_RPB_R = 2 * NA_WIN_R - 1
NA_RPB_C = 2 * NA_WIN_C - 1


def _na_key_row0(r0):
    return jnp.clip(r0 - NA_WIN_R // 2, 0, GRID_ROWS - NA_KROWS)


def _na_bias_table(rpb):
    lanes = 2 * GRID_W
    period = jnp.concatenate([rpb[..., NA_WIN_C - 1:],
                              jnp.zeros(rpb.shape[:2] + (lanes - NA_RPB_C,), rpb.dtype),
                              rpb[..., :NA_WIN_C - 1]], axis=-1)
    flat = jnp.tile(period, (1, 1, GRID_W))[..., :GRID_W * (lanes - 1)]
    toep = flat.reshape(N_HEADS, NA_RPB_R, GRID_W, lanes - 1)[..., :GRID_W].transpose(0, 2, 1, 3)

    pad_lo = NA_OFFS[-1] + NA_RB - NA_WIN_R
    pad_hi = NA_WIN_R - 1 + NA_KROWS - NA_RPB_R
    toep = jnp.pad(toep, ((0, 0), (0, 0), (pad_lo, pad_hi), (0, 0)))
    c = jnp.arange(GRID_W)[:, None]
    kc = jnp.arange(GRID_W)[None, :]
    cs = jnp.clip(c - NA_WIN_C // 2, 0, GRID_W - NA_WIN_C)
    col_ok = ((kc >= cs) & (kc < cs + NA_WIN_C))[None, :, None, :]
    jr = jnp.arange(NA_KROWS)[None, None, :, None]
    kinds = []
    for kind, off in enumerate(NA_OFFS):
        per_row = []
        for a in range(NA_RB):
            rs_rel = (0, a, NA_KROWS - NA_WIN_R)[kind]
            ok = (jr >= rs_rel) & (jr < rs_rel + NA_WIN_R) & col_ok
            first_off = NA_WIN_R - 1 - off - a + pad_lo
            per_row.append(jnp.where(ok, toep[:, :, first_off:first_off + NA_KROWS, :], NEG_INF))
        kinds.append(jnp.stack(per_row, axis=1))
    return jnp.stack(kinds, axis=0).reshape(len(NA_OFFS), N_HEADS, NA_Q, NA_KEYS).astype(F32)


_NT = (((1,), (1,)), ((), ()))


def _with_ones(v):
    return jnp.concatenate([v, jnp.ones_like(v)], axis=1)


def _na_kernel(q_ref, k_ref, v_ref, kc_ref, vc_ref, bias_ref, o_ref):
    kc = kc_ref[...]
    vc = vc_ref[...]

    def body(t, carry):
        r0 = t * NA_RB
        ks = _na_key_row0(r0)
        q0 = pl.multiple_of(r0 * GRID_W, NA_Q)
        k0 = pl.multiple_of(ks * GRID_W, GRID_W)
        q = q_ref[pl.ds(q0, NA_Q), :]
        kw = k_ref[pl.ds(k0, NA_KEYS), :]
        vw = v_ref[pl.ds(k0, NA_KEYS), :]
        bias = bias_ref[(r0 - ks) // NA_RB]
        s_loc = lax.dot_general(q, kw, _NT, preferred_element_type=F32) + bias
        s_ctx = lax.dot_general(q, kc, _NT, preferred_element_type=F32)
        m = jnp.maximum(jnp.max(s_loc, axis=-1, keepdims=True), jnp.max(s_ctx, axis=-1, keepdims=True))
        p_loc = jnp.exp(s_loc - m)
        p_ctx = jnp.exp(s_ctx - m)
        l = jnp.sum(p_loc, axis=-1, keepdims=True) + jnp.sum(p_ctx, axis=-1, keepdims=True)
        o = (jnp.dot(p_loc.astype(BF16), vw, preferred_element_type=F32)
             + jnp.dot(p_ctx.astype(BF16), vc, preferred_element_type=F32))
        o_ref[pl.ds(q0, NA_Q), :] = (o / l).astype(o_ref.dtype)
        return carry

    lax.fori_loop(0, GRID_ROWS // NA_RB, body, 0, unroll=2)


def _na_attention(qkv, bias_tab):
    h3 = N_HEADS
    ctx_blk = R_LAT // CTX_LEN
    return pl.pallas_call(
        _na_kernel,
        grid=(BATCH, N_HEADS),
        in_specs=[
            pl.BlockSpec((SEQ, HEAD_DIM), lambda b, h: (b, h)),
            pl.BlockSpec((SEQ, HEAD_DIM), lambda b, h: (b, h3 + h)),
            pl.BlockSpec((SEQ, HEAD_DIM), lambda b, h: (b, 2 * h3 + h)),
            pl.BlockSpec((CTX_LEN, HEAD_DIM), lambda b, h: (ctx_blk + b, h3 + h)),
            pl.BlockSpec((CTX_LEN, HEAD_DIM), lambda b, h: (ctx_blk + b, 2 * h3 + h)),
            pl.BlockSpec((len(NA_OFFS), None, NA_Q, NA_KEYS), lambda b, h: (0, h, 0, 0)),
        ],
        out_specs=pl.BlockSpec((SEQ, HEAD_DIM), lambda b, h: (b, h)),
        out_shape=jax.ShapeDtypeStruct((R_LAT, D_MODEL), BF16),
        compiler_params=_cparams(("parallel", "parallel")),
        name="na_attn",
    )(qkv, qkv, qkv, qkv, qkv, bias_tab)


def _ctx_attn_kernel(q_ref, k_ref, v_ref, o_ref):
    s = lax.dot_general(q_ref[...], k_ref[...], _NT, preferred_element_type=F32)
    p = jnp.exp(s - jnp.max(s, axis=-1, keepdims=True)).astype(BF16)
    ol = jnp.dot(p, _with_ones(v_ref[...]), preferred_element_type=F32)
    o_ref[...] = (ol[:, :HEAD_DIM] / ol[:, HEAD_DIM:]).astype(o_ref.dtype)


def _ctx_attention(qkv):
    h3 = N_HEADS
    ctx_blk = R_LAT // CTX_LEN
    return pl.pallas_call(
        _ctx_attn_kernel,
        grid=(BATCH, N_HEADS),
        in_specs=[
            pl.BlockSpec((CTX_LEN, HEAD_DIM), lambda b, h: (ctx_blk + b, h)),
            pl.BlockSpec((CTX_LEN, HEAD_DIM), lambda b, h: (ctx_blk + b, h3 + h)),
            pl.BlockSpec((CTX_LEN, HEAD_DIM), lambda b, h: (ctx_blk + b, 2 * h3 + h)),
        ],
        out_specs=pl.BlockSpec((CTX_LEN, HEAD_DIM), lambda b, h: (b, h)),
        out_shape=jax.ShapeDtypeStruct((R_CTX, D_MODEL), BF16),
        compiler_params=_cparams(("parallel", "parallel")),
        name="ctx_attn",
    )(qkv, qkv, qkv)


SWA_G = N_HEADS // N_KV_HEADS
SWA_KEYS = 3 * SWA_BLOCK


SWA_NQ = 4


def _swa_mask_table():
    shape = (3, SWA_G * SWA_BLOCK, SWA_KEYS)
    lead = lax.broadcasted_iota(jnp.int32, shape, 0) * SWA_BLOCK
    qoff = lax.broadcasted_iota(jnp.int32, shape, 1) % SWA_BLOCK
    koff = lax.broadcasted_iota(jnp.int32, shape, 2)
    return jnp.where(jnp.abs(koff - lead - qoff) <= SWA_WINDOW, 0.0, NEG_INF).astype(F32)


def _swa_kernel(q_ref, k_ref, v_ref, kc_ref, vc_ref, sink_ref, mask_ref, o_ref):
    kc = kc_ref[...]
    vc1 = _with_ones(vc_ref[...])
    sink = sink_ref[...]
    for qi in range(SWA_NQ):
        n = pl.program_id(2) * SWA_NQ + qi
        ws = pl.multiple_of(jnp.clip((n - 1) * SWA_BLOCK, 0, SEQ - SWA_KEYS), SWA_BLOCK)
        rows = slice(qi * SWA_BLOCK, (qi + 1) * SWA_BLOCK)
        q = jnp.concatenate([q_ref[rows, g * HEAD_DIM:(g + 1) * HEAD_DIM] for g in range(SWA_G)], axis=0)
        kw = k_ref[pl.ds(ws, SWA_KEYS), :]
        vw = v_ref[pl.ds(ws, SWA_KEYS), :]
        s_loc = lax.dot_general(q, kw, _NT, preferred_element_type=F32) + mask_ref[n - ws // SWA_BLOCK]
        s_ctx = lax.dot_general(q, kc, _NT, preferred_element_type=F32)
        m = jnp.maximum(jnp.maximum(jnp.max(s_loc, axis=-1, keepdims=True),
                                    jnp.max(s_ctx, axis=-1, keepdims=True)), sink)
        p_loc = jnp.exp(s_loc - m).astype(BF16)
        p_ctx = jnp.exp(s_ctx - m).astype(BF16)
        ol = (jnp.dot(p_loc, _with_ones(vw), preferred_element_type=F32)
              + jnp.dot(p_ctx, vc1, preferred_element_type=F32))
        o = ol[:, :HEAD_DIM] / (ol[:, HEAD_DIM:] + jnp.exp(sink - m))
        for g in range(SWA_G):
            o_ref[rows, g * HEAD_DIM:(g + 1) * HEAD_DIM] = o[g * SWA_BLOCK:(g + 1) * SWA_BLOCK].astype(o_ref.dtype)


def _swa_attention(qkv, sink_rows):
    nb = SEQ // (SWA_BLOCK * SWA_NQ)
    kcol = N_HEADS
    vcol = N_HEADS + N_KV_HEADS
    ctx_blk = R_LAT // CTX_LEN
    return pl.pallas_call(
        _swa_kernel,
        grid=(BATCH, N_KV_HEADS, nb),
        in_specs=[
            pl.BlockSpec((SWA_NQ * SWA_BLOCK, SWA_G * HEAD_DIM), lambda b, kv, n: (b * nb + n, kv)),
            pl.BlockSpec((SEQ, HEAD_DIM), lambda b, kv, n: (b, kcol + kv)),
            pl.BlockSpec((SEQ, HEAD_DIM), lambda b, kv, n: (b, vcol + kv)),
            pl.BlockSpec((CTX_LEN, HEAD_DIM), lambda b, kv, n: (ctx_blk + b, kcol + kv)),
            pl.BlockSpec((CTX_LEN, HEAD_DIM), lambda b, kv, n: (ctx_blk + b, vcol + kv)),
            pl.BlockSpec((None, SWA_G * SWA_BLOCK, 1), lambda b, kv, n: (kv, 0, 0)),
            pl.BlockSpec((3, SWA_G * SWA_BLOCK, SWA_KEYS), lambda b, kv, n: (0, 0, 0)),
        ],
        out_specs=pl.BlockSpec((SWA_NQ * SWA_BLOCK, SWA_G * HEAD_DIM), lambda b, kv, n: (b * nb + n, kv)),
        out_shape=jax.ShapeDtypeStruct((R_LAT, D_MODEL), BF16),
        compiler_params=_cparams(("parallel", "parallel", "arbitrary")),
        name="swa_attn",
    )(qkv, qkv, qkv, qkv, qkv, sink_rows, _swa_mask_table())


CONV_HALO = 16
CONV_LANES = 128
CONV_ROWS = 64
SUBLANES = 8


def _halo_flags(i):
    tiles_per_seq = SEQ // CONV_TILE
    latent = i < R_LAT // CONV_TILE
    has_prev = jnp.logical_and(latent, i % tiles_per_seq != 0)
    has_next = jnp.logical_and(latent, i % tiles_per_seq != tiles_per_seq - 1)
    return has_prev, has_next


def _stage_rows(prev_ref, cur_ref, next_ref, stage_ref, i):
    has_prev, has_next = _halo_flags(i)
    stage_ref[0:CONV_HALO, :] = jnp.where(has_prev, prev_ref[...], 0.0)
    stage_ref[CONV_HALO:CONV_HALO + CONV_TILE, :] = cur_ref[...]
    stage_ref[CONV_HALO + CONV_TILE:, :] = jnp.where(has_next, next_ref[...], 0.0)


def _dwconv(stage_ref, w_ref, width, emit):
    base = CONV_HALO - width // 2

    def lane_chunk(ci, carry):
        lanes = pl.ds(pl.multiple_of(ci * CONV_LANES, CONV_LANES), CONV_LANES)
        w = w_ref[:, lanes]
        for r0 in range(0, CONV_TILE, CONV_ROWS):
            acc = None
            for phase in range(SUBLANES):
                part = None
                for k in range(width):
                    if (base + k) % SUBLANES != phase:
                        continue
                    start = r0 + base + k - phase
                    term = stage_ref[start:start + CONV_ROWS + SUBLANES, lanes] * w[k:k + 1, :]
                    part = term if part is None else part + term
                if part is not None:
                    part = part[phase:phase + CONV_ROWS]
                    acc = part if acc is None else acc + part
            emit(r0, lanes, acc)
        return carry

    lax.fori_loop(0, D_MODEL // CONV_LANES, lane_chunk, 0)


def _cv_conv_kernel(prev_ref, cur_ref, next_ref, w_ref, b_ref, lg_ref, lb_ref, o_ref, stage_ref, y_ref):
    i = pl.program_id(0)
    _stage_rows(prev_ref, cur_ref, next_ref, stage_ref, i)

    def emit(r0, lanes, y):
        y_ref[r0:r0 + CONV_ROWS, lanes] = y + b_ref[:, lanes]

    _dwconv(stage_ref, w_ref, CONV_WIDTH, emit)
    y = y_ref[...]
    mu = jnp.mean(y, axis=-1, keepdims=True)
    yc = y - mu
    var = jnp.mean(yc * yc, axis=-1, keepdims=True)
    z = yc * lax.rsqrt(var + EPS) * lg_ref[...] + lb_ref[...]
    o_ref[...] = _silu(z).astype(o_ref.dtype)


def _halo_specs():
    per = CONV_TILE // CONV_HALO
    last = R_ALL // CONV_HALO - 1
    return [
        pl.BlockSpec((CONV_HALO, D_MODEL), lambda i: (jnp.maximum(i * per - 1, 0), 0)),
        pl.BlockSpec((CONV_TILE, D_MODEL), lambda i: (i, 0)),
        pl.BlockSpec((CONV_HALO, D_MODEL), lambda i: (jnp.minimum((i + 1) * per, last), 0)),
    ]


def _cv_conv(z, w_dw, b_dw, ln_g, ln_b):
    vec = pl.BlockSpec((1, D_MODEL), lambda i: (0, 0))
    return pl.pallas_call(
        _cv_conv_kernel,
        grid=(R_ALL // CONV_TILE,),
        in_specs=_halo_specs() + [pl.BlockSpec((CONV_WIDTH, D_MODEL), lambda i: (0, 0)), vec, vec, vec],
        out_specs=pl.BlockSpec((CONV_TILE, D_MODEL), lambda i: (i, 0)),
        out_shape=jax.ShapeDtypeStruct((R_ALL, D_MODEL), BF16),
        scratch_shapes=[pltpu.VMEM((CONV_TILE + 2 * CONV_HALO, D_MODEL), F32),
                        pltpu.VMEM((CONV_TILE, D_MODEL), F32)],
        compiler_params=_cparams(("parallel",)),
        name="cv_conv",
    )(z, z, z, w_dw, b_dw, ln_g, ln_b)


def _sc_conv_kernel(prev_ref, cur_ref, next_ref, bg_ref, w_ref, o_ref, stage_ref):
    i = pl.program_id(0)
    _stage_rows(prev_ref, cur_ref, next_ref, stage_ref, i)

    def emit(r0, lanes, y):
        o_ref[r0:r0 + CONV_ROWS, lanes] = (bg_ref[r0:r0 + CONV_ROWS, lanes].astype(F32) * y).astype(o_ref.dtype)

    _dwconv(stage_ref, w_ref, SHORT_CONV_WIDTH, emit)


def _sc_conv(bg, p, w_conv):
    return pl.pallas_call(
        _sc_conv_kernel,
        grid=(R_ALL // CONV_TILE,),
        in_specs=_halo_specs() + [pl.BlockSpec((CONV_TILE, D_MODEL), lambda i: (i, 0)),
                                  pl.BlockSpec((SHORT_CONV_WIDTH, D_MODEL), lambda i: (0, 0))],
        out_specs=pl.BlockSpec((CONV_TILE, D_MODEL), lambda i: (i, 0)),
        out_shape=jax.ShapeDtypeStruct((R_ALL, D_MODEL), BF16),
        scratch_shapes=[pltpu.VMEM((CONV_TILE + 2 * CONV_HALO, D_MODEL), F32)],
        compiler_params=_cparams(("parallel",)),
        name="sc_conv",
    )(p, p, p, bg, w_conv)


def _rope_tables():
    t = jnp.arange(SEQ)
    quarter = ROPE_AXIS_DIM // 2
    inv_freq = jnp.power(ROPE_BASE, -jnp.arange(quarter, dtype=F32) / quarter)
    ang_r = (t // GRID_W).astype(F32)[:, None] * inv_freq[None, :]
    ang_c = (t % GRID_W).astype(F32)[:, None] * inv_freq[None, :]
    cos = jnp.concatenate([jnp.cos(ang_r), jnp.cos(ang_r), jnp.cos(ang_c), jnp.cos(ang_c)], axis=-1)
    sin = jnp.concatenate([-jnp.sin(ang_r), jnp.sin(ang_r), -jnp.sin(ang_c), jnp.sin(ang_c)], axis=-1)
    cos = jnp.concatenate([cos, jnp.ones((R_CTX, HEAD_DIM), F32)], axis=0)
    sin = jnp.concatenate([sin, jnp.zeros((R_CTX, HEAD_DIM), F32)], axis=0)
    return cos, sin


def kernel(x, c, ctx, c_ctx, w_mod, b_mod, norm_g, ffn_w_in, ffn_w_out, na_w_qkv, na_w_o, na_rpb,
           cv_w_pw1, cv_b_pw1, cv_w_dw, cv_b_dw, cv_ln_g, cv_ln_b, cv_w_pw2, cv_b_pw2,
           sc_w_in, sc_w_conv, sc_w_out, swa_w_qkv, swa_w_o, swa_sink, final_g):
    D = D_MODEL
    xa = jnp.concatenate([x.reshape(R_LAT, D), ctx.reshape(R_CTX, D)], axis=0)
    cc = jnp.concatenate([c, c_ctx[None, :], jnp.zeros((MOD_ROWS - BATCH - 1, D), F32)], axis=0)
    mods = _mods(cc, w_mod, b_mod).reshape(DEPTH, MOD_ROWS, N_MOD, 1, D)
    bf = lambda w: w.astype(BF16)
    row = lambda v: v.reshape(1, -1)
    w_in = bf(ffn_w_in)
    w_out = bf(ffn_w_out)

    for i in range(DEPTH):
        kind = i % 4
        last = i == DEPTH - 1
        g = norm_g[i]
        xa = _ffn(xa, mods, i, 0, row(g[0]), w_in, w_out, R_ALL)

        if kind == 0:
            qkv = _proj(xa, mods, i, row(g[1]), bf(na_w_qkv[0]), D)
            o_lat = _na_attention(qkv, _na_bias_table(na_rpb[0]))
            o_ctx = _ctx_attention(qkv)
            xa = _outproj(o_lat, o_ctx, xa, mods, i, bf(na_w_o[0]), None, R_ALL)
        elif kind == 1:
            z = _proj_glu(xa, mods, i, row(g[1]), bf(cv_w_pw1[0]), row(cv_b_pw1[0]))
            zc = _cv_conv(z, cv_w_dw[0], row(cv_b_dw[0]), row(cv_ln_g[0]), row(cv_ln_b[0]))
            xa = _outproj(zc, None, xa, mods, i, bf(cv_w_pw2[0]), row(cv_b_pw2[0]), R_ALL)
        elif kind == 2:
            bg, p = _proj_sc(xa, mods, i, row(g[1]), bf(sc_w_in[0]))
            y = _sc_conv(bg, p, sc_w_conv[0])
            xa = _outproj(y, None, xa, mods, i, bf(sc_w_out[0]), None, R_ALL)
        else:
            cos_t, sin_t = _rope_tables()
            qkv = _proj_rope(xa, mods, i, row(g[1]), bf(swa_w_qkv[0]), cos_t, sin_t,
                             D, D + N_KV_HEADS * HEAD_DIM)
            sink_rows = jnp.repeat(swa_sink[0].reshape(N_KV_HEADS, SWA_G), SWA_BLOCK, axis=1)[..., None]
            o_lat = _swa_attention(qkv, sink_rows)
            xa = _outproj(o_lat, None, xa, mods, i, bf(swa_w_o[0]), None, R_LAT)

        rows = R_LAT if last else R_ALL
        xa = _ffn(xa, mods, i, 1, row(g[2]), w_in, w_out, rows, final_g=row(final_g) if last else None)

    return xa.reshape(BATCH, SEQ, D)
```

```python
import functools

import jax
import jax.numpy as jnp
from jax import lax
from jax.experimental import pallas as pl
from jax.experimental.pallas import tpu as pltpu

D_MODEL = 2048
BATCH = 4
SEQ = 4096
DEPTH = 4
GRID_W = 64
CTX_LEN = 256
N_HEADS = 16
HEAD_DIM = D_MODEL // N_HEADS
N_KV_HEADS = 4
NA_WIN_R = 8
NA_WIN_C = 16
CONV_WIDTH = 31
SHORT_CONV_WIDTH = 3
SWA_WINDOW = 128
SWA_BLOCK = 128
D_FF = ((8 * D_MODEL // 3 + 255) // 256) * 256
ROPE_BASE = 10000.0
ROPE_AXIS_DIM = HEAD_DIM // 2
N_MOD = 9
MACARON_WEIGHT = 0.5
EPS = 1e-6
NEG_INF = -1e30

R_LAT = BATCH * SEQ
R_CTX = BATCH * CTX_LEN
R_ALL = R_LAT + R_CTX
MOD_ROWS = 8
ATTN_SCALE = HEAD_DIM ** -0.5

F32 = jnp.float32
BF16 = jnp.bfloat16

VMEM_LIMIT_BYTES = 56 * 1024 * 1024

TM = 512
TM_FFN = 1024
TM_PROJ = 1024
TF = 512
CONV_TILE = 256


def _cparams(sem):
    return pltpu.CompilerParams(dimension_semantics=sem, vmem_limit_bytes=VMEM_LIMIT_BYTES)


def _mod_row(i, tm):
    return jnp.minimum(i // (SEQ // tm), BATCH)


def _mod_spec(layer, k, tm, tile=lambda i: i):
    return pl.BlockSpec((None, None, None, 1, D_MODEL),
                        lambda i, j: (layer, _mod_row(tile(i), tm), k, 0, 0))


def _pro_tile(n_tiles):
    return lambda i: jnp.minimum(i, n_tiles - 1)


def _mm_tile(i):
    return jnp.maximum(i - 1, 0)


def _mm_step(i, s):
    return jnp.where(i == 0, 0, s)


def _norm_chunk(x_ref, g_ref, sh_ref, sc_ref, h_ref, slot, n_chunks):
    rows = x_ref.shape[0] // n_chunks
    r = pl.multiple_of(jnp.minimum(pl.program_id(1), n_chunks - 1) * rows, rows)
    h = _norm_mod(x_ref[pl.ds(r, rows), :], g_ref[...], sh_ref[...], sc_ref[...])
    h_ref[slot, pl.ds(r, rows), :] = h.astype(BF16)


def _pipelined_rows(first_row, other_rows):
    i = pl.program_id(0)

    @pl.when(i == 0)
    def _():
        first_row()

    @pl.when(i > 0)
    def _():
        other_rows((i + 1) % 2, i % 2)


def _rms(x, g):
    return x * lax.rsqrt(jnp.mean(x * x, axis=-1, keepdims=True) + EPS) * g


def _norm_mod(x, g, shift, scale):
    return _rms(x, g) * (1.0 + scale) + shift


def _silu(x):
    return x * jax.nn.sigmoid(x)


def _mods_kernel(cc_ref, w_ref, b_ref, o_ref):
    s = _silu(cc_ref[...]).astype(BF16)
    o_ref[...] = jnp.dot(s, w_ref[...].astype(BF16), preferred_element_type=F32) + b_ref[...]


def _mods(cc, w_mod, b_mod):
    tn = 1024
    n = N_MOD * D_MODEL
    return pl.pallas_call(
        _mods_kernel,
        grid=(DEPTH, n // tn),
        in_specs=[
            pl.BlockSpec((MOD_ROWS, D_MODEL), lambda l, j: (0, 0)),
            pl.BlockSpec((None, D_MODEL, tn), lambda l, j: (l, 0, j)),
            pl.BlockSpec((None, 1, tn), lambda l, j: (l, 0, j)),
        ],
        out_specs=pl.BlockSpec((None, MOD_ROWS, tn), lambda l, j: (l, 0, j)),
        out_shape=jax.ShapeDtypeStruct((DEPTH, MOD_ROWS, n), F32),
        compiler_params=_cparams(("parallel", "parallel")),
        name="mods",
    )(cc, w_mod, b_mod.reshape(DEPTH, 1, n))


FFN_CHUNKS = 8


def _ffn_kernel(*refs, final):
    if final:
        xc_ref, g_ref, sh_ref, sc_ref, gt_ref, wg_ref, wu_ref, wo_ref, fg_ref, o_ref, h_ref, xk_ref = refs
    else:
        xc_ref, g_ref, sh_ref, sc_ref, gt_ref, wg_ref, wu_ref, wo_ref, o_ref, h_ref, xk_ref = refs
    i = pl.program_id(0)
    f = pl.program_id(1)
    rows = xc_ref.shape[0]

    def prologue_chunk(slot):
        r = pl.multiple_of(jnp.minimum(f, FFN_CHUNKS - 1) * rows, rows)
        xs = xc_ref[...]
        xk_ref[pl.ds(r, rows), :] = xs
        h_ref[slot, pl.ds(r, rows), :] = _norm_mod(xs, g_ref[...], sh_ref[...], sc_ref[...]).astype(BF16)

    def matmul_row(read_slot, write_slot):
        @pl.when(f == 0)
        def _():
            o_ref[...] = xk_ref[...]

        h = h_ref[read_slot]
        gate = jnp.dot(h, wg_ref[...], preferred_element_type=F32)
        up = jnp.dot(h, wu_ref[...], preferred_element_type=F32)
        a = (_silu(gate) * up).astype(BF16)
        o_ref[...] += (MACARON_WEIGHT * gt_ref[...]) * jnp.dot(a, wo_ref[...], preferred_element_type=F32)
        prologue_chunk(write_slot)

        if final:
            @pl.when(f == pl.num_programs(1) - 1)
            def _():
                o_ref[...] = _rms(o_ref[...], fg_ref[...])

    _pipelined_rows(lambda: prologue_chunk(0), matmul_row)


def _ffn(x, mods, layer, half, g, w_in, w_out, rows, final_g=None):
    tm, tf = TM_FFN, TF
    nfc = D_FF // tf
    assert nfc >= FFN_CHUNKS
    nt = rows // tm
    k0 = 6 * half
    final = final_g is not None
    pro = _pro_tile(nt)
    in_specs = [
        pl.BlockSpec((tm // FFN_CHUNKS, D_MODEL),
                     lambda i, f: (pro(i) * FFN_CHUNKS + jnp.minimum(f, FFN_CHUNKS - 1), 0)),
        pl.BlockSpec((1, D_MODEL), lambda i, f: (0, 0)),
        _mod_spec(layer, k0, tm, pro),
        _mod_spec(layer, k0 + 1, tm, pro),
        _mod_spec(layer, k0 + 2, tm, _mm_tile),
        pl.BlockSpec((None, None, D_MODEL, tf), lambda i, f: (layer, half, 0, _mm_step(i, f))),
        pl.BlockSpec((None, None, D_MODEL, tf), lambda i, f: (layer, half, 0, nfc + _mm_step(i, f))),
        pl.BlockSpec((None, None, tf, D_MODEL), lambda i, f: (layer, half, _mm_step(i, f), 0)),
    ]
    args = [x, g, mods, mods, mods, w_in, w_in, w_out]
    if final:
        in_specs.append(pl.BlockSpec((1, D_MODEL), lambda i, f: (0, 0)))
        args.append(final_g)
    return pl.pallas_call(
        functools.partial(_ffn_kernel, final=final),
        grid=(nt + 1, nfc),
        in_specs=in_specs,
        out_specs=pl.BlockSpec((tm, D_MODEL), lambda i, f: (_mm_tile(i), 0)),
        out_shape=jax.ShapeDtypeStruct((rows, D_MODEL), F32),
        scratch_shapes=[pltpu.VMEM((2, tm, D_MODEL), BF16), pltpu.VMEM((tm, D_MODEL), F32)],
        compiler_params=_cparams(("arbitrary", "arbitrary")),
        name="ffn_final" if final else "ffn",
    )(*args)


PROJ_CHUNKS = 4


def _proj_pipeline(x_ref, g_ref, sh_ref, sc_ref, h_ref, matmuls):
    def chunk(slot):
        return _norm_chunk(x_ref, g_ref, sh_ref, sc_ref, h_ref, slot, PROJ_CHUNKS)

    _pipelined_rows(lambda: chunk(0),
                    lambda read_slot, write_slot: matmuls(h_ref[read_slot], lambda: chunk(write_slot)))


def _q_scale(n_q):
    return jnp.where(pl.program_id(1) < n_q, ATTN_SCALE, 1.0).astype(F32)


def _proj_kernel(x_ref, g_ref, sh_ref, sc_ref, w_ref, o_ref, h_ref, *, n_q):
    def matmuls(h, chunk):
        y = jnp.dot(h, w_ref[...], preferred_element_type=F32)
        o_ref[...] = (y * _q_scale(n_q)).astype(o_ref.dtype)
        chunk()

    _proj_pipeline(x_ref, g_ref, sh_ref, sc_ref, h_ref, matmuls)


def _rope_swap(y):
    n = y.shape[-1]
    lane = lax.broadcasted_iota(jnp.int32, y.shape, 1)
    return jnp.where(lane % ROPE_AXIS_DIM < ROPE_AXIS_DIM // 2,
                     pltpu.roll(y, n - ROPE_AXIS_DIM // 2, 1),
                     pltpu.roll(y, ROPE_AXIS_DIM // 2, 1))


def _proj_rope_kernel(x_ref, g_ref, sh_ref, sc_ref, w_ref, cos_ref, sin_ref, o_ref, h_ref, *, n_q, n_rope):
    def matmuls(h, chunk):
        j = pl.program_id(1)
        y = jnp.dot(h, w_ref[...], preferred_element_type=F32)
        chunk()

        @pl.when(j < n_rope)
        def _():
            scale = _q_scale(n_q)
            cos = cos_ref[...] * scale
            sin = sin_ref[...] * scale
            for hh in range(y.shape[-1] // HEAD_DIM):
                yh = y[:, hh * HEAD_DIM:(hh + 1) * HEAD_DIM]
                o_ref[:, hh * HEAD_DIM:(hh + 1) * HEAD_DIM] = (yh * cos + _rope_swap(yh) * sin).astype(o_ref.dtype)

        @pl.when(j >= n_rope)
        def _():
            o_ref[...] = y.astype(o_ref.dtype)

    _proj_pipeline(x_ref, g_ref, sh_ref, sc_ref, h_ref, matmuls)


PROJ_TILES = R_ALL // TM_PROJ


def _proj_common_specs(layer):
    pro = _pro_tile(PROJ_TILES)
    return [
        pl.BlockSpec((TM_PROJ, D_MODEL), lambda i, j: (pro(i), 0)),
        pl.BlockSpec((1, D_MODEL), lambda i, j: (0, 0)),
        _mod_spec(layer, 3, TM_PROJ, pro),
        _mod_spec(layer, 4, TM_PROJ, pro),
    ]


def _proj_col_spec(rows, tn, first=0):
    return pl.BlockSpec((rows, tn), lambda i, j: (0, first + _mm_step(i, j)))


def _proj_out_spec(tn):
    return pl.BlockSpec((TM_PROJ, tn), lambda i, j: (_mm_tile(i), _mm_step(i, j)))


def _proj_call(kernel, name, tn, n_steps, in_specs, out_specs, out_shape, args):
    assert n_steps >= PROJ_CHUNKS
    return pl.pallas_call(
        kernel,
        grid=(PROJ_TILES + 1, n_steps),
        in_specs=in_specs,
        out_specs=out_specs,
        out_shape=out_shape,
        scratch_shapes=[pltpu.VMEM((2, TM_PROJ, D_MODEL), BF16)],
        compiler_params=_cparams(("arbitrary", "arbitrary")),
        name=name,
    )(*args)


def _proj(x, mods, layer, g, w, n_q_cols):
    tn = 1024
    n = w.shape[1]
    return _proj_call(
        functools.partial(_proj_kernel, n_q=n_q_cols // tn), "proj", tn, n // tn,
        _proj_common_specs(layer) + [_proj_col_spec(D_MODEL, tn)],
        _proj_out_spec(tn), jax.ShapeDtypeStruct((R_ALL, n), BF16),
        (x, g, mods, mods, w))


def _proj_rope(x, mods, layer, g, w, cos_t, sin_t, n_q_cols, n_rope_cols):
    tn = 512
    n = w.shape[1]
    n_lat = R_LAT // TM_PROJ
    tpb = SEQ // TM_PROJ

    def tab(i, j):
        t = _mm_tile(i)
        return (jnp.where(t < n_lat, t % tpb, tpb + t - n_lat), 0)

    return _proj_call(
        functools.partial(_proj_rope_kernel, n_q=n_q_cols // tn, n_rope=n_rope_cols // tn),
        "proj_rope", tn, n // tn,
        _proj_common_specs(layer) + [_proj_col_spec(D_MODEL, tn),
                                     pl.BlockSpec((TM_PROJ, HEAD_DIM), tab),
                                     pl.BlockSpec((TM_PROJ, HEAD_DIM), tab)],
        _proj_out_spec(tn), jax.ShapeDtypeStruct((R_ALL, n), BF16),
        (x, g, mods, mods, w, cos_t, sin_t))


def _proj_glu_kernel(x_ref, g_ref, sh_ref, sc_ref, wa_ref, wg_ref, ba_ref, bg_ref, o_ref, h_ref):
    def matmuls(h, chunk):
        a = jnp.dot(h, wa_ref[...], preferred_element_type=F32) + ba_ref[...]
        gt = jnp.dot(h, wg_ref[...], preferred_element_type=F32) + bg_ref[...]
        o_ref[...] = a * jax.nn.sigmoid(gt)
        chunk()

    _proj_pipeline(x_ref, g_ref, sh_ref, sc_ref, h_ref, matmuls)


def _proj_glu(x, mods, layer, g, w, b):
    tn = 512
    nc = D_MODEL // tn
    return _proj_call(
        _proj_glu_kernel, "proj_glu", tn, nc,
        _proj_common_specs(layer) + [_proj_col_spec(D_MODEL, tn), _proj_col_spec(D_MODEL, tn, nc),
                                     _proj_col_spec(1, tn), _proj_col_spec(1, tn, nc)],
        _proj_out_spec(tn), jax.ShapeDtypeStruct((R_ALL, D_MODEL), F32),
        (x, g, mods, mods, w, w, b, b))


def _proj_sc_kernel(x_ref, g_ref, sh_ref, sc_ref, wb_ref, wc_ref, wx_ref, bg_ref, p_ref, h_ref):
    def matmuls(h, chunk):
        bg_ref[...] = jnp.dot(h, wb_ref[...], preferred_element_type=F32).astype(bg_ref.dtype)
        cg = jnp.dot(h, wc_ref[...], preferred_element_type=F32)
        xin = jnp.dot(h, wx_ref[...], preferred_element_type=F32)
        p_ref[...] = cg * xin
        chunk()

    _proj_pipeline(x_ref, g_ref, sh_ref, sc_ref, h_ref, matmuls)


def _proj_sc(x, mods, layer, g, w):
    tn = 512
    nc = D_MODEL // tn
    return _proj_call(
        _proj_sc_kernel, "proj_sc", tn, nc,
        _proj_common_specs(layer) + [_proj_col_spec(D_MODEL, tn), _proj_col_spec(D_MODEL, tn, nc),
                                     _proj_col_spec(D_MODEL, tn, 2 * nc)],
        [_proj_out_spec(tn), _proj_out_spec(tn)],
        [jax.ShapeDtypeStruct((R_ALL, D_MODEL), BF16), jax.ShapeDtypeStruct((R_ALL, D_MODEL), F32)],
        (x, g, mods, mods, w, w, w))


def _outproj_kernel(*refs, n_lat, has_ctx, has_bias):
    refs = list(refs)
    ol_ref = refs.pop(0)
    oc_ref = refs.pop(0) if has_ctx else None
    w_ref = refs.pop(0)
    b_ref = refs.pop(0) if has_bias else None
    x_ref, gt_ref, o_ref = refs
    i = pl.program_id(0)

    def finish(src_ref):
        y = jnp.dot(src_ref[...], w_ref[...], preferred_element_type=F32)
        if has_bias:
            y = y + b_ref[...]
        o_ref[...] = x_ref[...] + gt_ref[...] * y

    if has_ctx:
        @pl.when(i < n_lat)
        def _():
            finish(ol_ref)

        @pl.when(i >= n_lat)
        def _():
            finish(oc_ref)
    else:
        finish(ol_ref)


def _outproj(o_lat, o_ctx, x, mods, layer, w, bias, rows):
    tm, tn = TM, D_MODEL
    n_lat = R_LAT // tm
    has_ctx = o_ctx is not None
    has_bias = bias is not None
    in_specs = []
    args = []
    if has_ctx:
        in_specs.append(pl.BlockSpec((tm, D_MODEL), lambda i, j: (jnp.minimum(i, n_lat - 1), 0)))
        in_specs.append(pl.BlockSpec((tm, D_MODEL), lambda i, j: (jnp.maximum(i - n_lat, 0), 0)))
        args += [o_lat, o_ctx]
    else:
        in_specs.append(pl.BlockSpec((tm, D_MODEL), lambda i, j: (i, 0)))
        args.append(o_lat)
    in_specs.append(pl.BlockSpec((D_MODEL, tn), lambda i, j: (0, j)))
    args.append(w)
    if has_bias:
        in_specs.append(pl.BlockSpec((1, tn), lambda i, j: (0, j)))
        args.append(bias)
    in_specs.append(pl.BlockSpec((tm, tn), lambda i, j: (i, j)))
    in_specs.append(_mod_spec(layer, 5, tm))
    args += [x, mods]
    return pl.pallas_call(
        functools.partial(_outproj_kernel, n_lat=n_lat, has_ctx=has_ctx, has_bias=has_bias),
        grid=(rows // tm, D_MODEL // tn),
        in_specs=in_specs,
        out_specs=pl.BlockSpec((tm, tn), lambda i, j: (i, j)),
        out_shape=jax.ShapeDtypeStruct((rows, D_MODEL), F32),
        compiler_params=_cparams(("parallel", "parallel")),
        name="outproj",
    )(*args)


GRID_ROWS = SEQ // GRID_W
NA_RB = 4
NA_KROWS = 12
NA_Q = NA_RB * GRID_W
NA_KEYS = NA_KROWS * GRID_W
NA_OFFS = (0, NA_RB, 2 * NA_RB)
NA_RPB_R = 2 * NA_WIN_R - 1
NA_RPB_C = 2 * NA_WIN_C - 1


def _na_key_row0(r0):
    return jnp.clip(r0 - NA_WIN_R // 2, 0, GRID_ROWS - NA_KROWS)


NA_PAD_LO = NA_OFFS[-1] + NA_RB - NA_WIN_R
NA_PAD_HI = NA_WIN_R - 1 + NA_KROWS - NA_RPB_R
NA_PAIRS = NA_PAD_LO + NA_RPB_R + NA_PAD_HI - 1


def _na_toeplitz_pairs(rpb):
    lanes = 2 * GRID_W
    period = jnp.concatenate([rpb[..., NA_WIN_C - 1:],
                              jnp.zeros(rpb.shape[:2] + (lanes - NA_RPB_C,), rpb.dtype),
                              rpb[..., :NA_WIN_C - 1]], axis=-1)
    flat = jnp.tile(period, (1, 1, GRID_W))[..., :GRID_W * (lanes - 1)]
    toep = flat.reshape(N_HEADS, NA_RPB_R, GRID_W, lanes - 1)[..., :GRID_W]
    toep = jnp.pad(toep, ((0, 0), (NA_PAD_LO, NA_PAD_HI), (0, 0), (0, 0)))
    return jnp.concatenate([toep[:, :-1], toep[:, 1:]], axis=-1).astype(F32)


_NT = (((1,), (1,)), ((), ()))


def _with_ones(v):
    return jnp.concatenate([v, jnp.ones_like(v)], axis=1)


def _na_build_bias(pairs_ref, bias_ref):
    shape = (GRID_W, 2 * GRID_W)
    c = lax.broadcasted_iota(jnp.int32, shape, 0)
    lane = lax.broadcasted_iota(jnp.int32, shape, 1)
    kc = lane % GRID_W
    cs = jnp.clip(c - NA_WIN_C // 2, 0, GRID_W - NA_WIN_C)
    col_ok = (kc >= cs) & (kc < cs + NA_WIN_C)
    half_ok = {(True, True): col_ok,
               (True, False): col_ok & (lane < GRID_W),
               (False, True): col_ok & (lane >= GRID_W)}
    for kind, off in enumerate(NA_OFFS):
        for a in range(NA_RB):
            rs_rel = (0, a, NA_KROWS - NA_WIN_R)[kind]
            first_off = NA_WIN_R - 1 - off - a + NA_PAD_LO
            for jj in range(NA_KROWS // 2):
                rows_ok = tuple(rs_rel <= jr < rs_rel + NA_WIN_R for jr in (2 * jj, 2 * jj + 1))
                if any(rows_ok):
                    piece = jnp.where(half_ok[rows_ok], pairs_ref[first_off + 2 * jj], NEG_INF)
                else:
                    piece = jnp.full(shape, NEG_INF, F32)
                bias_ref[kind, a * GRID_W:(a + 1) * GRID_W, jj * 2 * GRID_W:(jj + 1) * 2 * GRID_W] = piece


def _na_kernel(q_ref, k_ref, v_ref, kc_ref, vc_ref, pairs_ref, o_ref, bias_ref):
    kc = kc_ref[...]
    vc = vc_ref[...]
    _na_build_bias(pairs_ref, bias_ref)

    def body(t, carry):
        r0 = t * NA_RB
        ks = _na_key_row0(r0)
        q0 = pl.multiple_of(r0 * GRID_W, NA_Q)
        k0 = pl.multiple_of(ks * GRID_W, GRID_W)
        q = q_ref[pl.ds(q0, NA_Q), :]
        kw = k_ref[pl.ds(k0, NA_KEYS), :]
        vw = v_ref[pl.ds(k0, NA_KEYS), :]
        bias = bias_ref[(r0 - ks) // NA_RB]
        s_loc = lax.dot_general(q, kw, _NT, preferred_element_type=F32) + bias
        s_ctx = lax.dot_general(q, kc, _NT, preferred_element_type=F32)
        m = jnp.maximum(jnp.max(s_loc, axis=-1, keepdims=True), jnp.max(s_ctx, axis=-1, keepdims=True))
        p_loc = jnp.exp(s_loc - m)
        p_ctx = jnp.exp(s_ctx - m)
        l = jnp.sum(p_loc, axis=-1, keepdims=True) + jnp.sum(p_ctx, axis=-1, keepdims=True)
        o = (jnp.dot(p_loc.astype(BF16), vw, preferred_element_type=F32)
             + jnp.dot(p_ctx.astype(BF16), vc, preferred_element_type=F32))
        o_ref[pl.ds(q0, NA_Q), :] = (o / l).astype(o_ref.dtype)
        return carry

    lax.fori_loop(0, GRID_ROWS // NA_RB, body, 0, unroll=4)


def _na_attention(qkv, pairs):
    h3 = N_HEADS
    ctx_blk = R_LAT // CTX_LEN
    return pl.pallas_call(
        _na_kernel,
        grid=(BATCH, N_HEADS),
        in_specs=[
            pl.BlockSpec((SEQ, HEAD_DIM), lambda b, h: (b, h)),
            pl.BlockSpec((SEQ, HEAD_DIM), lambda b, h: (b, h3 + h)),
            pl.BlockSpec((SEQ, HEAD_DIM), lambda b, h: (b, 2 * h3 + h)),
            pl.BlockSpec((CTX_LEN, HEAD_DIM), lambda b, h: (ctx_blk + b, h3 + h)),
            pl.BlockSpec((CTX_LEN, HEAD_DIM), lambda b, h: (ctx_blk + b, 2 * h3 + h)),
            pl.BlockSpec((None, NA_PAIRS, GRID_W, 2 * GRID_W), lambda b, h: (h, 0, 0, 0)),
        ],
        out_specs=pl.BlockSpec((SEQ, HEAD_DIM), lambda b, h: (b, h)),
        out_shape=jax.ShapeDtypeStruct((R_LAT, D_MODEL), BF16),
        scratch_shapes=[pltpu.VMEM((len(NA_OFFS), NA_Q, NA_KEYS), F32)],
        compiler_params=_cparams(("parallel", "parallel")),
        name="na_attn",
    )(qkv, qkv, qkv, qkv, qkv, pairs)


def _ctx_attn_kernel(q_ref, k_ref, v_ref, o_ref):
    s = lax.dot_general(q_ref[...], k_ref[...], _NT, preferred_element_type=F32)
    p = jnp.exp(s - jnp.max(s, axis=-1, keepdims=True)).astype(BF16)
    ol = jnp.dot(p, _with_ones(v_ref[...]), preferred_element_type=F32)
    o_ref[...] = (ol[:, :HEAD_DIM] / ol[:, HEAD_DIM:]).astype(o_ref.dtype)


def _ctx_attention(qkv):
    h3 = N_HEADS
    ctx_blk = R_LAT // CTX_LEN
    return pl.pallas_call(
        _ctx_attn_kernel,
        grid=(BATCH, N_HEADS),
        in_specs=[
            pl.BlockSpec((CTX_LEN, HEAD_DIM), lambda b, h: (ctx_blk + b, h)),
            pl.BlockSpec((CTX_LEN, HEAD_DIM), lambda b, h: (ctx_blk + b, h3 + h)),
            pl.BlockSpec((CTX_LEN, HEAD_DIM), lambda b, h: (ctx_blk + b, 2 * h3 + h)),
        ],
        out_specs=pl.BlockSpec((CTX_LEN, HEAD_DIM), lambda b, h: (b, h)),
        out_shape=jax.ShapeDtypeStruct((R_CTX, D_MODEL), BF16),
        compiler_params=_cparams(("parallel", "parallel")),
        name="ctx_attn",
    )(qkv, qkv, qkv)


SWA_G = N_HEADS // N_KV_HEADS
SWA_KEYS = 3 * SWA_BLOCK


SWA_NQ = 4


def _swa_mask_table():
    shape = (3, SWA_G * SWA_BLOCK, SWA_KEYS)
    lead = lax.broadcasted_iota(jnp.int32, shape, 0) * SWA_BLOCK
    qoff = lax.broadcasted_iota(jnp.int32, shape, 1) % SWA_BLOCK
    koff = lax.broadcasted_iota(jnp.int32, shape, 2)
    return jnp.where(jnp.abs(koff - lead - qoff) <= SWA_WINDOW, 0.0, NEG_INF).astype(F32)


def _swa_kernel(q_ref, k_ref, v_ref, kc_ref, vc_ref, sink_ref, mask_ref, o_ref):
    kc = kc_ref[...]
    vc1 = _with_ones(vc_ref[...])
    sink = sink_ref[...]
    for qi in range(SWA_NQ):
        n = pl.program_id(2) * SWA_NQ + qi
        ws = pl.multiple_of(jnp.clip((n - 1) * SWA_BLOCK, 0, SEQ - SWA_KEYS), SWA_BLOCK)
        rows = slice(qi * SWA_BLOCK, (qi + 1) * SWA_BLOCK)
        q = jnp.concatenate([q_ref[rows, g * HEAD_DIM:(g + 1) * HEAD_DIM] for g in range(SWA_G)], axis=0)
        kw = k_ref[pl.ds(ws, SWA_KEYS), :]
        vw = v_ref[pl.ds(ws, SWA_KEYS), :]
        s_loc = lax.dot_general(q, kw, _NT, preferred_element_type=F32) + mask_ref[n - ws // SWA_BLOCK]
        s_ctx = lax.dot_general(q, kc, _NT, preferred_element_type=F32)
        m = jnp.maximum(jnp.maximum(jnp.max(s_loc, axis=-1, keepdims=True),
                                    jnp.max(s_ctx, axis=-1, keepdims=True)), sink)
        p_loc = jnp.exp(s_loc - m).astype(BF16)
        p_ctx = jnp.exp(s_ctx - m).astype(BF16)
        ol = (jnp.dot(p_loc, _with_ones(vw), preferred_element_type=F32)
              + jnp.dot(p_ctx, vc1, preferred_element_type=F32))
        o = ol[:, :HEAD_DIM] / (ol[:, HEAD_DIM:] + jnp.exp(sink - m))
        for g in range(SWA_G):
            o_ref[rows, g * HEAD_DIM:(g + 1) * HEAD_DIM] = o[g * SWA_BLOCK:(g + 1) * SWA_BLOCK].astype(o_ref.dtype)


def _swa_attention(qkv, sink_rows):
    nb = SEQ // (SWA_BLOCK * SWA_NQ)
    kcol = N_HEADS
    vcol = N_HEADS + N_KV_HEADS
    ctx_blk = R_LAT // CTX_LEN
    return pl.pallas_call(
        _swa_kernel,
        grid=(BATCH, N_KV_HEADS, nb),
        in_specs=[
            pl.BlockSpec((SWA_NQ * SWA_BLOCK, SWA_G * HEAD_DIM), lambda b, kv, n: (b * nb + n, kv)),
            pl.BlockSpec((SEQ, HEAD_DIM), lambda b, kv, n: (b, kcol + kv)),
            pl.BlockSpec((SEQ, HEAD_DIM), lambda b, kv, n: (b, vcol + kv)),
            pl.BlockSpec((CTX_LEN, HEAD_DIM), lambda b, kv, n: (ctx_blk + b, kcol + kv)),
            pl.BlockSpec((CTX_LEN, HEAD_DIM), lambda b, kv, n: (ctx_blk + b, vcol + kv)),
            pl.BlockSpec((None, SWA_G * SWA_BLOCK, 1), lambda b, kv, n: (kv, 0, 0)),
            pl.BlockSpec((3, SWA_G * SWA_BLOCK, SWA_KEYS), lambda b, kv, n: (0, 0, 0)),
        ],
        out_specs=pl.BlockSpec((SWA_NQ * SWA_BLOCK, SWA_G * HEAD_DIM), lambda b, kv, n: (b * nb + n, kv)),
        out_shape=jax.ShapeDtypeStruct((R_LAT, D_MODEL), BF16),
        compiler_params=_cparams(("parallel", "parallel", "arbitrary")),
        name="swa_attn",
    )(qkv, qkv, qkv, qkv, qkv, sink_rows, _swa_mask_table())


CONV_HALO = 16
CONV_LANES = 128
CONV_ROWS = 64
SUBLANES = 8


def _halo_flags(i):
    tiles_per_seq = SEQ // CONV_TILE
    latent = i < R_LAT // CONV_TILE
    has_prev = jnp.logical_and(latent, i % tiles_per_seq != 0)
    has_next = jnp.logical_and(latent, i % tiles_per_seq != tiles_per_seq - 1)
    return has_prev, has_next


def _stage_rows(prev_ref, cur_ref, next_ref, stage_ref, i):
    has_prev, has_next = _halo_flags(i)
    stage_ref[0:CONV_HALO, :] = jnp.where(has_prev, prev_ref[...], 0.0)
    stage_ref[CONV_HALO:CONV_HALO + CONV_TILE, :] = cur_ref[...]
    stage_ref[CONV_HALO + CONV_TILE:, :] = jnp.where(has_next, next_ref[...], 0.0)


def _dwconv(stage_ref, w_ref, width, emit):
    base = CONV_HALO - width // 2

    def lane_chunk(ci, carry):
        lanes = pl.ds(pl.multiple_of(ci * CONV_LANES, CONV_LANES), CONV_LANES)
        w = w_ref[:, lanes]
        for r0 in range(0, CONV_TILE, CONV_ROWS):
            acc = None
            for phase in range(SUBLANES):
                part = None
                for k in range(width):
                    if (base + k) % SUBLANES != phase:
                        continue
                    start = r0 + base + k - phase
                    term = stage_ref[start:start + CONV_ROWS + SUBLANES, lanes] * w[k:k + 1, :]
                    part = term if part is None else part + term
                if part is not None:
                    part = part[phase:phase + CONV_ROWS]
                    acc = part if acc is None else acc + part
            emit(r0, lanes, acc)
        return carry

    lax.fori_loop(0, D_MODEL // CONV_LANES, lane_chunk, 0)


def _cv_conv_kernel(prev_ref, cur_ref, next_ref, w_ref, b_ref, lg_ref, lb_ref, o_ref, stage_ref, y_ref):
    i = pl.program_id(0)
    _stage_rows(prev_ref, cur_ref, next_ref, stage_ref, i)

    def emit(r0, lanes, y):
        y_ref[r0:r0 + CONV_ROWS, lanes] = y + b_ref[:, lanes]

    _dwconv(stage_ref, w_ref, CONV_WIDTH, emit)
    y = y_ref[...]
    mu = jnp.mean(y, axis=-1, keepdims=True)
    yc = y - mu
    var = jnp.mean(yc * yc, axis=-1, keepdims=True)
    z = yc * lax.rsqrt(var + EPS) * lg_ref[...] + lb_ref[...]
    o_ref[...] = _silu(z).astype(o_ref.dtype)


def _halo_specs():
    per = CONV_TILE // CONV_HALO
    last = R_ALL // CONV_HALO - 1
    return [
        pl.BlockSpec((CONV_HALO, D_MODEL), lambda i: (jnp.maximum(i * per - 1, 0), 0)),
        pl.BlockSpec((CONV_TILE, D_MODEL), lambda i: (i, 0)),
        pl.BlockSpec((CONV_HALO, D_MODEL), lambda i: (jnp.minimum((i + 1) * per, last), 0)),
    ]


def _cv_conv(z, w_dw, b_dw, ln_g, ln_b):
    vec = pl.BlockSpec((1, D_MODEL), lambda i: (0, 0))
    return pl.pallas_call(
        _cv_conv_kernel,
        grid=(R_ALL // CONV_TILE,),
        in_specs=_halo_specs() + [pl.BlockSpec((CONV_WIDTH, D_MODEL), lambda i: (0, 0)), vec, vec, vec],
        out_specs=pl.BlockSpec((CONV_TILE, D_MODEL), lambda i: (i, 0)),
        out_shape=jax.ShapeDtypeStruct((R_ALL, D_MODEL), BF16),
        scratch_shapes=[pltpu.VMEM((CONV_TILE + 2 * CONV_HALO, D_MODEL), F32),
                        pltpu.VMEM((CONV_TILE, D_MODEL), F32)],
        compiler_params=_cparams(("parallel",)),
        name="cv_conv",
    )(z, z, z, w_dw, b_dw, ln_g, ln_b)


def _sc_conv_kernel(prev_ref, cur_ref, next_ref, bg_ref, w_ref, o_ref, stage_ref):
    i = pl.program_id(0)
    _stage_rows(prev_ref, cur_ref, next_ref, stage_ref, i)

    def emit(r0, lanes, y):
        o_ref[r0:r0 + CONV_ROWS, lanes] = (bg_ref[r0:r0 + CONV_ROWS, lanes].astype(F32) * y).astype(o_ref.dtype)

    _dwconv(stage_ref, w_ref, SHORT_CONV_WIDTH, emit)


def _sc_conv(bg, p, w_conv):
    return pl.pallas_call(
        _sc_conv_kernel,
        grid=(R_ALL // CONV_TILE,),
        in_specs=_halo_specs() + [pl.BlockSpec((CONV_TILE, D_MODEL), lambda i: (i, 0)),
                                  pl.BlockSpec((SHORT_CONV_WIDTH, D_MODEL), lambda i: (0, 0))],
        out_specs=pl.BlockSpec((CONV_TILE, D_MODEL), lambda i: (i, 0)),
        out_shape=jax.ShapeDtypeStruct((R_ALL, D_MODEL), BF16),
        scratch_shapes=[pltpu.VMEM((CONV_TILE + 2 * CONV_HALO, D_MODEL), F32)],
        compiler_params=_cparams(("parallel",)),
        name="sc_conv",
    )(p, p, p, bg, w_conv)


def _rope_tables():
    t = jnp.arange(SEQ)
    quarter = ROPE_AXIS_DIM // 2
    inv_freq = jnp.power(ROPE_BASE, -jnp.arange(quarter, dtype=F32) / quarter)
    ang_r = (t // GRID_W).astype(F32)[:, None] * inv_freq[None, :]
    ang_c = (t % GRID_W).astype(F32)[:, None] * inv_freq[None, :]
    cos = jnp.concatenate([jnp.cos(ang_r), jnp.cos(ang_r), jnp.cos(ang_c), jnp.cos(ang_c)], axis=-1)
    sin = jnp.concatenate([-jnp.sin(ang_r), jnp.sin(ang_r), -jnp.sin(ang_c), jnp.sin(ang_c)], axis=-1)
    cos = jnp.concatenate([cos, jnp.ones((R_CTX, HEAD_DIM), F32)], axis=0)
    sin = jnp.concatenate([sin, jnp.zeros((R_CTX, HEAD_DIM), F32)], axis=0)
    return cos, sin


def kernel(x, c, ctx, c_ctx, w_mod, b_mod, norm_g, ffn_w_in, ffn_w_out, na_w_qkv, na_w_o, na_rpb,
           cv_w_pw1, cv_b_pw1, cv_w_dw, cv_b_dw, cv_ln_g, cv_ln_b, cv_w_pw2, cv_b_pw2,
           sc_w_in, sc_w_conv, sc_w_out, swa_w_qkv, swa_w_o, swa_sink, final_g):
    D = D_MODEL
    xa = jnp.concatenate([x.reshape(R_LAT, D), ctx.reshape(R_CTX, D)], axis=0)
    cc = jnp.concatenate([c, c_ctx[None, :], jnp.zeros((MOD_ROWS - BATCH - 1, D), F32)], axis=0)
    mods = _mods(cc, w_mod, b_mod).reshape(DEPTH, MOD_ROWS, N_MOD, 1, D)
    bf = lambda w: w.astype(BF16)
    row = lambda v: v.reshape(1, -1)
    w_in = bf(ffn_w_in)
    w_out = bf(ffn_w_out)

    for i in range(DEPTH):
        kind = i % 4
        last = i == DEPTH - 1
        g = norm_g[i]
        xa = _ffn(xa, mods, i, 0, row(g[0]), w_in, w_out, R_ALL)

        if kind == 0:
            qkv = _proj(xa, mods, i, row(g[1]), bf(na_w_qkv[0]), D)
            o_lat = _na_attention(qkv, _na_toeplitz_pairs(na_rpb[0]))
            o_ctx = _ctx_attention(qkv)
            xa = _outproj(o_lat, o_ctx, xa, mods, i, bf(na_w_o[0]), None, R_ALL)
        elif kind == 1:
            z = _proj_glu(xa, mods, i, row(g[1]), bf(cv_w_pw1[0]), row(cv_b_pw1[0]))
            zc = _cv_conv(z, cv_w_dw[0], row(cv_b_dw[0]), row(cv_ln_g[0]), row(cv_ln_b[0]))
            xa = _outproj(zc, None, xa, mods, i, bf(cv_w_pw2[0]), row(cv_b_pw2[0]), R_ALL)
        elif kind == 2:
            bg, p = _proj_sc(xa, mods, i, row(g[1]), bf(sc_w_in[0]))
            y = _sc_conv(bg, p, sc_w_conv[0])
            xa = _outproj(y, None, xa, mods, i, bf(sc_w_out[0]), None, R_ALL)
        else:
            cos_t, sin_t = _rope_tables()
            qkv = _proj_rope(xa, mods, i, row(g[1]), bf(swa_w_qkv[0]), cos_t, sin_t,
                             D, D + N_KV_HEADS * HEAD_DIM)
            sink_rows = jnp.repeat(swa_sink[0].reshape(N_KV_HEADS, SWA_G), SWA_BLOCK, axis=1)[..., None]
            o_lat = _swa_attention(qkv, sink_rows)
            xa = _outproj(o_lat, None, xa, mods, i, bf(swa_w_o[0]), None, R_LAT)

        rows = R_LAT if last else R_ALL
        xa = _ffn(xa, mods, i, 1, row(g[2]), w_in, w_out, rows, final_g=row(final_g) if last else None)

    return xa.reshape(BATCH, SEQ, D)
```
